```python
import functools
import jax
import jax.numpy as jnp
from jax import lax
import numpy as np

D_MODEL = 1024
BATCH = 2
SEQ = 8192
DEPTH = 4
DEC_BATCH = 128
DEC_SEQ = 1
PAST_LEN = 2048
PAGE_SIZE = 128

HEAD_DIM = 64
MIX_W = D_MODEL
RWKV_W = MIX_W // 2
RWKV_H = RWKV_W // HEAD_DIM
ATTN_W = MIX_W - RWKV_W
ATTN_H = ATTN_W // HEAD_DIM
KV_H = ATTN_H
W_LORA = 64
A_LORA = 64
G_LORA = 128
IDX_H = 8
IDX_D = 32
TOPK_MAX = 256
ROT_DIM = HEAD_DIM // 4
IDX_ROT_DIM = IDX_D // 4
ROPE_THETA = 500000.0
Q_BLOCK = 128
D_FF = 2816
N_EXPERTS = 8
TOP_K_EXPERTS = 2
D_FF_EXPERT = 1408
NORM_EPS = 1e-6
LNX_EPS = 64e-5
N_DENSE = (DEPTH + 1) // 2
N_MOE = DEPTH // 2
SHIFT_W = 3 * RWKV_W + W_LORA + A_LORA + G_LORA
ATTN_IN_W = ATTN_W + 2 * KV_H * HEAD_DIM
IDX_IN_W = IDX_H * IDX_D + IDX_D + IDX_H
IN_W = SHIFT_W + ATTN_IN_W + IDX_IN_W
F32 = jnp.float32

kernel_name = 'hymba_rwkv7_dsa_moe_decode_step'


def rmsnorm(x, w):
    xf = x.astype(F32)
    y = xf * lax.rsqrt(jnp.mean(xf * xf, axis=-1, keepdims=True) + NORM_EPS)
    return (y * w.astype(F32)).astype(x.dtype)


def rope_partial(x, pos, rot_dim):
    half = rot_dim // 2
    inv_freq = ROPE_THETA ** (-jnp.arange(half, dtype=F32) / half)
    ang = pos.astype(F32)[:, None] * inv_freq[None, :]
    cos = jnp.cos(ang)[None, :, None, :]
    sin = jnp.sin(ang)[None, :, None, :]
    xr = x[..., :rot_dim].astype(F32)
    x1, x2 = xr[..., :half], xr[..., half:]
    rot = jnp.concatenate([x1 * cos - x2 * sin, x1 * sin + x2 * cos], axis=-1).astype(x.dtype)
    return jnp.concatenate([rot, x[..., rot_dim:]], axis=-1)


def adaln(c, w_ada, b_ada):
    mod = jax.nn.silu(c) @ w_ada + b_ada
    return jnp.split(mod[:, None, :], 6, axis=-1)


def swiglu(x, wg, wu, wd):
    return (jax.nn.silu(x @ wg) * (x @ wu)) @ wd


def moe_swiglu(x, router, wg, wu, wd):
    b, t, d = x.shape
    xf = x.reshape(b * t, d)
    logits = (xf @ router).astype(F32)
    top_v, top_i = lax.top_k(logits, TOP_K_EXPERTS)
    gates = jax.nn.softmax(top_v, axis=-1)
    combine = jnp.sum(jax.nn.one_hot(top_i, N_EXPERTS, dtype=F32) * gates[..., None], axis=1)
    out = jnp.zeros_like(xf)
    for e in range(N_EXPERTS):
        out = out + combine[:, e:e + 1].astype(x.dtype) * swiglu(xf, wg[e], wu[e], wd[e])
    return out.reshape(b, t, d)


def rwkv7_mixer(p, shift_prev, wkv0, mu, w0, w2, a0, a2, g2, k_k, k_a, r_k, lnx_w, lnx_b):
    b, t, _ = p.shape
    prev = jnp.concatenate([shift_prev[:, None, :].astype(p.dtype), p[:, :-1]], axis=1)
    xs = p + mu * (prev - p)
    r = xs[..., :RWKV_W]
    k = xs[..., RWKV_W:2 * RWKV_W]
    v = xs[..., 2 * RWKV_W:3 * RWKV_W]
    o1 = 3 * RWKV_W
    wl = xs[..., o1:o1 + W_LORA]
    al = xs[..., o1 + W_LORA:o1 + W_LORA + A_LORA]
    gl = xs[..., o1 + W_LORA + A_LORA:]
    w = -jax.nn.softplus(-(w0 + jnp.tanh(wl) @ w2)) - 0.5
    decay = jnp.exp(-jnp.exp(w.astype(F32)))
    a = jax.nn.sigmoid(a0 + al @ a2)
    g = jax.nn.sigmoid(gl) @ g2
    heads = lambda z: z.reshape(b, t, RWKV_H, HEAD_DIM).astype(F32)
    kk = heads(k * k_k)
    kk = kk * lax.rsqrt(jnp.maximum(jnp.sum(kk * kk, axis=-1, keepdims=True), 1e-24))
    k = k * (1 + (a - 1) * k_a)
    rh, kh, vh, ah, dh = heads(r), heads(k), heads(v), heads(a), heads(decay)

    def step(S, inp):
        r_t, d_t, k_t, v_t, kk_t, a_t = inp
        sk = jnp.einsum('bhvk,bhk->bhv', S, kk_t)
        S = (S * d_t[:, :, None, :] - sk[..., :, None] * (kk_t * a_t)[..., None, :]
             + v_t[..., :, None] * k_t[..., None, :])
        return S, jnp.einsum('bhvk,bhk->bhv', S, r_t)

    seq = tuple(jnp.moveaxis(z, 1, 0) for z in (rh, dh, kh, vh, kk, ah))
    S_final, o = lax.scan(step, wkv0.astype(F32), seq)
    o = jnp.moveaxis(o, 0, 1)
    mean = jnp.mean(o, axis=-1, keepdims=True)
    var = jnp.mean(jnp.square(o - mean), axis=-1, keepdims=True)
    o = ((o - mean) * lax.rsqrt(var + LNX_EPS)).reshape(b, t, RWKV_W)
    o = o * lnx_w.astype(F32) + lnx_b.astype(F32)
    bonus = jnp.sum(rh * kh * r_k.astype(F32), axis=-1, keepdims=True) * vh
    out = (o + bonus.reshape(b, t, RWKV_W)) * g.astype(F32)
    return out.astype(p.dtype), S_final.astype(wkv0.dtype)


def indexer_scores(q_idx, w_idx, k_idx):
    dots = jnp.einsum('bqhd,bsd->bqhs', q_idx.astype(F32), k_idx.astype(F32))
    return jnp.einsum('bqh,bqhs->bqs', w_idx.astype(F32), jax.nn.relu(dots))


def sparse_attend(q, k_sel, v_sel, valid):
    s = jnp.einsum('bqhd,bqkhd->bqhk', q.astype(F32), k_sel.astype(F32)) * (HEAD_DIM ** -0.5)
    s = jnp.where(valid[:, :, None, :], s, -jnp.inf)
    p = jax.nn.softmax(s, axis=-1)
    return jnp.einsum('bqhk,bqkhd->bqhd', p.astype(v_sel.dtype), v_sel)


def dsa_prompt(q, k, v, q_idx, k_idx, w_idx):
    b, t = q.shape[:2]
    k_sel = min(TOPK_MAX, t // 4)
    nb = t // Q_BLOCK
    key_pos = jnp.arange(t, dtype=jnp.int32)

    def blockify(z):
        return jnp.moveaxis(z.reshape((b, nb, Q_BLOCK) + z.shape[2:]), 1, 0)

    def one_block(args):
        qb, qib, wib, qpos = args
        sc = indexer_scores(qib, wib, k_idx)
        causal = key_pos[None, :] <= qpos[:, None]
        sc = jnp.where(causal[None], sc, -jnp.inf)
        _, idx = lax.top_k(sc, k_sel)
        k_g = jax.vmap(lambda kb, ib: kb[ib])(k, idx)
        v_g = jax.vmap(lambda vb, ib: vb[ib])(v, idx)
        valid = idx <= qpos[None, :, None]
        return sparse_attend(qb, k_g, v_g, valid)

    out = lax.map(one_block, (blockify(q), blockify(q_idx), blockify(w_idx),
                              key_pos.reshape(nb, Q_BLOCK)))
    return jnp.moveaxis(out, 0, 1).reshape(b, t, ATTN_H, HEAD_DIM)


def dsa_sample(q, k, v, q_idx, k_idx, w_idx, cache_k_l, cache_v_l, cache_kidx_l, page_table):
    bd, tn = q.shape[:2]
    past = page_table.shape[1] * PAGE_SIZE
    n_keys = past + tn
    k_sel = min(TOPK_MAX, n_keys // 4)
    past_kidx = cache_kidx_l[page_table].reshape(bd, past, IDX_D)
    all_kidx = jnp.concatenate([past_kidx, k_idx.astype(past_kidx.dtype)], axis=1)
    qpos = past + jnp.arange(tn, dtype=jnp.int32)
    sc = indexer_scores(q_idx, w_idx, all_kidx)
    causal = jnp.arange(n_keys, dtype=jnp.int32)[None, :] <= qpos[:, None]
    sc = jnp.where(causal[None], sc, -jnp.inf)
    _, idx = lax.top_k(sc, k_sel)
    is_new = (idx >= past)[..., None, None]
    pidx = jnp.minimum(idx, past - 1)
    b_ix = jnp.arange(bd, dtype=jnp.int32)[:, None, None]
    phys = page_table[b_ix, pidx // PAGE_SIZE]
    off = pidx % PAGE_SIZE
    nidx = jnp.clip(idx - past, 0, tn - 1)
    k_g = jnp.where(is_new, k[b_ix, nidx].astype(cache_k_l.dtype), cache_k_l[phys, off])
    v_g = jnp.where(is_new, v[b_ix, nidx].astype(cache_v_l.dtype), cache_v_l[phys, off])
    valid = idx <= qpos[None, :, None]
    return sparse_attend(q, k_g, v_g, valid)


def trunk_layer(x, c, pos, shift_prev, wkv0, attend, ffn, lp):
    b, t, _ = x.shape
    sh1, sc1, gt1, sh2, sc2, gt2 = adaln(c, lp['w_ada'], lp['b_ada'])
    h = rmsnorm(x, lp['norm1_w']) * (1 + sc1) + sh1
    proj = h @ lp['w_in']
    p_rwkv = proj[..., :SHIFT_W]
    p_attn = proj[..., SHIFT_W:SHIFT_W + ATTN_IN_W]
    p_idx = proj[..., SHIFT_W + ATTN_IN_W:]
    o_rwkv, wkv_new = rwkv7_mixer(p_rwkv, shift_prev, wkv0, lp['mu'], lp['w0'], lp['w2'], lp['a0'],
                                  lp['a2'], lp['g2'], lp['k_k'], lp['k_a'], lp['r_k'],
                                  lp['lnx_w'], lp['lnx_b'])
    kv_w = KV_H * HEAD_DIM
    q = p_attn[..., :ATTN_W].reshape(b, t, ATTN_H, HEAD_DIM)
    k = p_attn[..., ATTN_W:ATTN_W + kv_w].reshape(b, t, KV_H, HEAD_DIM)
    v = p_attn[..., ATTN_W + kv_w:].reshape(b, t, KV_H, HEAD_DIM)
    q = rope_partial(rmsnorm(q, lp['q_norm_w']), pos, ROT_DIM)
    k = rope_partial(rmsnorm(k, lp['k_norm_w']), pos, ROT_DIM)
    iq = IDX_H * IDX_D
    q_idx = rope_partial(p_idx[..., :iq].reshape(b, t, IDX_H, IDX_D), pos, IDX_ROT_DIM)
    k_idx = rope_partial(p_idx[..., iq:iq + IDX_D][:, :, None, :], pos, IDX_ROT_DIM)[:, :, 0, :]
    w_idx = p_idx[..., iq + IDX_D:] * (iq ** -0.5)
    o_attn = attend(q, k, v, q_idx, k_idx, w_idx)
    mix = jnp.concatenate([o_rwkv, o_attn.reshape(b, t, ATTN_W).astype(o_rwkv.dtype)], axis=-1)
    x = x + gt1 * (mix @ lp['w_out'])
    h2 = rmsnorm(x, lp['norm2_w']) * (1 + sc2) + sh2
    x = x + gt2 * ffn(h2)
    return x, (k, v, k_idx, wkv_new, p_rwkv[:, -1])


def setup_inputs(seed: int = 0) -> dict:
    key = jax.random.key(seed)
    keys = iter(jax.random.split(key, 64))

    def nrm(shape, scale):
        return jax.random.normal(next(keys), shape, F32) * scale

    n_pages = PAST_LEN // PAGE_SIZE
    n_used = DEC_BATCH * n_pages
    n_pool = n_used + (n_used + 3) // 4
    perm = jax.random.permutation(next(keys), n_pool)
    page_table = perm[:n_used].reshape(DEC_BATCH, n_pages).astype(jnp.int32)
    d = D_MODEL
    return {
        'x_prompt': nrm((BATCH, SEQ, d), 1.0),
        'x_sample': nrm((DEC_BATCH, DEC_SEQ, d), 1.0),
        'c_prompt': nrm((BATCH, d), 1.0),
        'c_sample': nrm((DEC_BATCH, d), 1.0),
        'cache_k': nrm((DEPTH, n_pool, PAGE_SIZE, KV_H, HEAD_DIM), 1.0),
        'cache_v': nrm((DEPTH, n_pool, PAGE_SIZE, KV_H, HEAD_DIM), 1.0),
        'cache_kidx': nrm((DEPTH, n_pool, PAGE_SIZE, IDX_D), 1.0),
        'state_wkv': nrm((DEPTH, DEC_BATCH, RWKV_H, HEAD_DIM, HEAD_DIM), 0.1),
        'state_shift': nrm((DEPTH, DEC_BATCH, SHIFT_W), 1.0),
        'page_table': page_table,
        'norm1_w': 1.0 + nrm((DEPTH, d), 0.05),
        'norm2_w': 1.0 + nrm((DEPTH, d), 0.05),
        'w_ada': nrm((DEPTH, d, 6 * d), 0.5 * d ** -0.5),
        'b_ada': nrm((DEPTH, 6 * d), 0.02),
        'w_in': nrm((DEPTH, d, IN_W), d ** -0.5),
        'mu_shift': jax.random.uniform(next(keys), (DEPTH, SHIFT_W), F32),
        'rwkv_w0': nrm((DEPTH, RWKV_W), 0.5),
        'rwkv_w2': nrm((DEPTH, W_LORA, RWKV_W), 0.5 * W_LORA ** -0.5),
        'rwkv_a0': nrm((DEPTH, RWKV_W), 0.1),
        'rwkv_a2': nrm((DEPTH, A_LORA, RWKV_W), 0.5 * A_LORA ** -0.5),
        'rwkv_g2': nrm((DEPTH, G_LORA, RWKV_W), G_LORA ** -0.5),
        'rwkv_k_k': 0.85 + nrm((DEPTH, RWKV_W), 0.05),
        'rwkv_k_a': 1.0 + nrm((DEPTH, RWKV_W), 0.05),
        'rwkv_r_k': nrm((DEPTH, RWKV_H, HEAD_DIM), 0.1),
        'lnx_w': 1.0 + nrm((DEPTH, RWKV_W), 0.05),
        'lnx_b': nrm((DEPTH, RWKV_W), 0.02),
        'q_norm_w': 1.0 + nrm((DEPTH, HEAD_DIM), 0.05),
        'k_norm_w': 1.0 + nrm((DEPTH, HEAD_DIM), 0.05),
        'w_out': nrm((DEPTH, MIX_W, d), MIX_W ** -0.5),
        'ffn_w_gate': nrm((N_DENSE, d, D_FF), d ** -0.5),
        'ffn_w_up': nrm((N_DENSE, d, D_FF), d ** -0.5),
        'ffn_w_down': nrm((N_DENSE, D_FF, d), D_FF ** -0.5),
        'router_w': nrm((N_MOE, d, N_EXPERTS), d ** -0.5),
        'exp_w_gate': nrm((N_MOE, N_EXPERTS, d, D_FF_EXPERT), d ** -0.5),
        'exp_w_up': nrm((N_MOE, N_EXPERTS, d, D_FF_EXPERT), d ** -0.5),
        'exp_w_down': nrm((N_MOE, N_EXPERTS, D_FF_EXPERT, d), D_FF_EXPERT ** -0.5),
    }


def reference(x_prompt, x_sample, c_prompt, c_sample, cache_k, cache_v, cache_kidx, state_wkv,
              state_shift, page_table, norm1_w, norm2_w, w_ada, b_ada, w_in, mu_shift, rwkv_w0,
              rwkv_w2, rwkv_a0, rwkv_a2, rwkv_g2, rwkv_k_k, rwkv_k_a, rwkv_r_k, lnx_w, lnx_b,
              q_norm_w, k_norm_w, w_out, ffn_w_gate, ffn_w_up, ffn_w_down, router_w, exp_w_gate,
              exp_w_up, exp_w_down):
    pos_p = jnp.arange(x_prompt.shape[1], dtype=jnp.int32)
    pos_s = PAST_LEN + jnp.arange(x_sample.shape[1], dtype=jnp.int32)
    xp, xs = x_prompt, x_sample
    bp = x_prompt.shape[0]
    st_p, st_s = [], []
    for l in range(DEPTH):
        lp = {'norm1_w': norm1_w[l], 'norm2_w': norm2_w[l], 'w_ada': w_ada[l], 'b_ada': b_ada[l],
              'w_in': w_in[l], 'mu': mu_shift[l], 'w0': rwkv_w0[l], 'w2': rwkv_w2[l],
              'a0': rwkv_a0[l], 'a2': rwkv_a2[l], 'g2': rwkv_g2[l], 'k_k': rwkv_k_k[l],
              'k_a': rwkv_k_a[l], 'r_k': rwkv_r_k[l], 'lnx_w': lnx_w[l], 'lnx_b': lnx_b[l],
              'q_norm_w': q_norm_w[l], 'k_norm_w': k_norm_w[l], 'w_out': w_out[l]}
        j = l // 2
        if l % 2 == 0:
            ffn = functools.partial(swiglu, wg=ffn_w_gate[j], wu=ffn_w_up[j], wd=ffn_w_down[j])
        else:
            ffn = functools.partial(moe_swiglu, router=router_w[j], wg=exp_w_gate[j],
                                    wu=exp_w_up[j], wd=exp_w_down[j])
        shift0 = jnp.zeros((bp, SHIFT_W), xp.dtype)
        wkv0 = jnp.zeros((bp, RWKV_H, HEAD_DIM, HEAD_DIM), F32)
        xp, sp = trunk_layer(xp, c_prompt, pos_p, shift0, wkv0, dsa_prompt, ffn, lp)
        attend_s = functools.partial(dsa_sample, cache_k_l=cache_k[l], cache_v_l=cache_v[l],
                                     cache_kidx_l=cache_kidx[l], page_table=page_table)
        xs, ss = trunk_layer(xs, c_sample, pos_s, state_shift[l], state_wkv[l], attend_s, ffn, lp)
        st_p.append(sp)
        st_s.append(ss)
    k_prompt = jnp.stack([s[0] for s in st_p])
    v_prompt = jnp.stack([s[1] for s in st_p])
    kidx_prompt = jnp.stack([s[2] for s in st_p])
    wkv_prompt = jnp.stack([s[3] for s in st_p])
    shift_prompt = jnp.stack([s[4] for s in st_p])
    k_sample = jnp.stack([s[0] for s in st_s])
    v_sample = jnp.stack([s[1] for s in st_s])
    kidx_sample = jnp.stack([s[2] for s in st_s])
    wkv_sample = jnp.stack([s[3] for s in st_s])
    shift_sample = jnp.stack([s[4] for s in st_s])
    return (xp, xs, k_prompt, v_prompt, kidx_prompt, wkv_prompt, shift_prompt,
            k_sample, v_sample, kidx_sample, wkv_sample, shift_sample)
```

```python
import functools

import jax
import jax.numpy as jnp
from jax import lax
from jax.experimental import pallas as pl
from jax.experimental.pallas import tpu as pltpu

F32 = jnp.float32
BF16 = jnp.bfloat16
I32 = jnp.int32

D_MODEL = 1024
HEAD_DIM = 64
RWKV_W = 512
RWKV_H = 8
ATTN_W = 512
ATTN_H = 8
W_LORA = 64
A_LORA = 64
G_LORA = 128
LORA_W = W_LORA + A_LORA + G_LORA
IDX_H = 8
IDX_D = 32
IDX_QW = IDX_H * IDX_D
TOPK_MAX = 256
ROT_DIM = 16
IDX_ROT_DIM = 8
ROPE_THETA = 500000.0
PAGE_SIZE = 128
D_FF_TILE = 1408
N_EXPERTS = 8
NORM_EPS = 1e-6
LNX_EPS = 64e-5
SHIFT_W = 3 * RWKV_W + LORA_W
ATTN_IN_W = ATTN_W + 2 * ATTN_W
IDX_IN_W = IDX_QW + IDX_D + IDX_H
IDX_PAD_W = 384
IN_W = SHIFT_W + ATTN_IN_W + IDX_IN_W

LANES = 128
VMEM_LIMIT = 56 * 1024 * 1024
INT_MIN = -(2 ** 31)


def _cparams(*sem):
    return pltpu.CompilerParams(dimension_semantics=sem, vmem_limit_bytes=VMEM_LIMIT)


def _mm(a, b):
    return jnp.dot(a.astype(BF16), b.astype(BF16), preferred_element_type=F32)


def _split2(a):
    hi = a.astype(BF16)
    lo = (a - hi.astype(F32)).astype(BF16)
    return hi, lo


def _mm_lhs2(a, b_bf16):
    hi, lo = _split2(a)
    return (jnp.dot(hi, b_bf16, preferred_element_type=F32)
            + jnp.dot(lo, b_bf16, preferred_element_type=F32))


def _mm3(a, b):
    a_hi, a_lo = _split2(a)
    b_hi, b_lo = _split2(b)
    return (jnp.dot(a_hi, b_hi, preferred_element_type=F32)
            + jnp.dot(a_lo, b_hi, preferred_element_type=F32)
            + jnp.dot(a_hi, b_lo, preferred_element_type=F32))


def _silu(x):
    return x * jax.nn.sigmoid(x)


def _seg_ones(width, seg, scale=1.0):
    r = lax.broadcasted_iota(I32, (width, width), 0) // seg
    c = lax.broadcasted_iota(I32, (width, width), 1) // seg
    return jnp.where(r == c, scale, 0.0).astype(BF16)


def _ada_kernel(c_ref, w_ref, b_ref, o_ref):
    o_ref[...] = _mm3(_silu(c_ref[...]), w_ref[...]) + b_ref[...]


def ada_modulation(c_all, w_ada, b_ada):
    m, d = c_all.shape
    nl, _, n = w_ada.shape
    tn = 1536
    return pl.pallas_call(
        _ada_kernel,
        grid=(nl, n // tn),
        in_specs=[
            pl.BlockSpec((m, d), lambda l, j: (0, 0)),
            pl.BlockSpec((None, d, tn), lambda l, j: (l, 0, j)),
            pl.BlockSpec((None, 1, tn), lambda l, j: (l, 0, j)),
        ],
        out_specs=pl.BlockSpec((None, m, tn), lambda l, j: (l, 0, j)),
        out_shape=jax.ShapeDtypeStruct((nl, m, n), F32),
        compiler_params=_cparams("parallel", "parallel"),
        name="ada_modulation",
    )(c_all, w_ada, b_ada.reshape(nl, 1, n))


def _modulated_norm(x, norm_w, shift, scale):
    y = x * lax.rsqrt(jnp.mean(x * x, axis=-1, keepdims=True) + NORM_EPS)
    return (y * norm_w) * (1.0 + scale) + shift


def _proj_in_kernel(x_ref, nw_ref, mod_ref, w_ref, pr_ref, pa_ref, pi_ref):
    d = D_MODEL
    h = _modulated_norm(x_ref[...], nw_ref[...], mod_ref[:, 0:d], mod_ref[:, d:2 * d])
    p = jnp.dot(h.astype(BF16), w_ref[...], preferred_element_type=F32)
    pr_ref[...] = p[:, :SHIFT_W]
    pa_ref[...] = p[:, SHIFT_W:SHIFT_W + ATTN_IN_W]
    pi_ref[...] = p[:, SHIFT_W + ATTN_IN_W:]


def _mod_spec(mod, tm):
    if mod.shape[1] == 1:
        return pl.BlockSpec((None, 1, mod.shape[2]), lambda g, i: (g, 0, 0))
    return pl.BlockSpec((None, tm, mod.shape[2]), lambda g, i: (g, i, 0))


def proj_in(x, norm_w, mod, w_in_bf16, tm):
    g, r, d = x.shape
    n = w_in_bf16.shape[1]
    widths = (SHIFT_W, ATTN_IN_W, n - SHIFT_W - ATTN_IN_W)
    return pl.pallas_call(
        _proj_in_kernel,
        grid=(g, r // tm),
        in_specs=[
            pl.BlockSpec((None, tm, d), lambda g, i: (g, i, 0)),
            pl.BlockSpec((1, d), lambda g, i: (0, 0)),
            _mod_spec(mod, tm),
            pl.BlockSpec((d, n), lambda g, i: (0, 0)),
        ],
        out_specs=[pl.BlockSpec((None, tm, w), lambda g, i: (g, i, 0)) for w in widths],
        out_shape=[jax.ShapeDtypeStruct((g, r, w), F32) for w in widths],
        compiler_params=_cparams("parallel", "parallel"),
        name="proj_in",
    )(x, norm_w.reshape(1, d), mod, w_in_bf16)


def _rwkv_prep_kernel(seq_mode, p_ref, halo_ref, sp_ref, mu_ref, vec_ref, w2_ref, a2_ref, g2_ref,
                      r_ref, d_ref, k_ref, v_ref, kk_ref, b_ref, g_ref, bonus_ref):
    p = p_ref[...]
    if seq_mode:
        first = jnp.where(pl.program_id(1) == 0, sp_ref[...], halo_ref[7:8, :])
        row = lax.broadcasted_iota(I32, p.shape, 0)
        prev = jnp.where(row == 0, first, pltpu.roll(p, 1, axis=0))
    else:
        prev = sp_ref[...]
    xs = p + mu_ref[...] * (prev - p)
    w = RWKV_W
    r = xs[:, 0:w]
    k = xs[:, w:2 * w]
    v = xs[:, 2 * w:3 * w]
    lo = xs[:, 3 * w:]
    lane = lax.broadcasted_iota(I32, lo.shape, 1)
    act = jnp.where(lane < W_LORA, jnp.tanh(lo),
                    jnp.where(lane < W_LORA + A_LORA, lo, jax.nn.sigmoid(lo)))
    w0, a0, k_k, k_a, r_k = (vec_ref[i:i + 1, :] for i in range(5))
    z = -(w0 + _mm3(act, w2_ref[...]))
    softplus = jnp.maximum(z, 0.0) + jnp.log(1.0 + jnp.exp(-jnp.abs(z)))
    decay = jnp.exp(-jnp.exp(-softplus - 0.5))
    a = jax.nn.sigmoid(a0 + _mm3(act, a2_ref[...]))
    g = _mm3(act, g2_ref[...])
    seg = _seg_ones(w, HEAD_DIM)
    kk = k * k_k
    kk = kk * lax.rsqrt(jnp.maximum(_mm_lhs2(kk * kk, seg), 1e-24))
    k2 = k * (1.0 + (a - 1.0) * k_a)
    r_ref[...] = r
    d_ref[...] = decay
    k_ref[...] = k2
    v_ref[...] = v
    kk_ref[...] = kk
    b_ref[...] = kk * a
    g_ref[...] = g
    bonus_ref[...] = _mm_lhs2(r * k2 * r_k, seg) * v


def rwkv_prep(p, shift_prev, mu, vecs, w2p, a2p, g2p, tm, seq_mode):
    g, r, sw = p.shape
    w = RWKV_W
    tok = lambda width: pl.BlockSpec((None, tm, width), lambda g, i: (g, i, 0))
    full = lambda a: pl.BlockSpec(a.shape, lambda g, i: (0,) * a.ndim)
    if seq_mode:
        halo_spec = pl.BlockSpec((None, 8, sw), lambda g, i: (g, jnp.maximum(i * (tm // 8) - 1, 0), 0))
        sp_spec = pl.BlockSpec((None, 1, sw), lambda g, i: (g, 0, 0))
    else:
        halo_spec = pl.BlockSpec((None, 8, sw), lambda g, i: (g, 0, 0))
        sp_spec = tok(sw)
    return pl.pallas_call(
        functools.partial(_rwkv_prep_kernel, seq_mode),
        grid=(g, r // tm),
        in_specs=[tok(sw), halo_spec, sp_spec, full(mu), full(vecs), full(w2p), full(a2p), full(g2p)],
        out_specs=[tok(w)] * 8,
        out_shape=[jax.ShapeDtypeStruct((g, r, w), F32)] * 8,
        compiler_params=_cparams("parallel", "arbitrary"),
        name="rwkv_prep",
    )(p, p, shift_prev, mu, vecs, w2p, a2p, g2p)


def _rwkv_scan_kernel(bb_n, tt, r_ref, d_ref, k_ref, kk_ref, b_ref, vt_ref, s0_ref, ot_ref, st_ref):
    @pl.when(pl.program_id(1) == 0)
    def _():
        st_ref[...] = s0_ref[...]

    ot_ref[...] = jnp.zeros_like(ot_ref)
    lane_a = lax.broadcasted_iota(I32, (HEAD_DIM, LANES), 1) < HEAD_DIM
    t_lane = lax.broadcasted_iota(I32, (HEAD_DIM, tt), 1)

    def pair_sum(x):
        s_a = jnp.sum(jnp.where(lane_a, x, 0.0), axis=1, keepdims=True)
        s_b = jnp.sum(jnp.where(lane_a, 0.0, x), axis=1, keepdims=True)
        return s_a, s_b

    def step(t, rows_of, u):
        hot = t_lane == t
        for bb in range(bb_n):
            for j in range(RWKV_H // 2):
                ls = slice(j * LANES, (j + 1) * LANES)
                row = lambda ref: rows_of(ref, bb, ls)[u:u + 1, :]
                s = st_ref[bb, j]
                sk_a, sk_b = pair_sum(s * row(kk_ref))
                v_a = jnp.sum(jnp.where(hot, vt_ref[bb, 2 * j], 0.0), axis=1, keepdims=True)
                v_b = jnp.sum(jnp.where(hot, vt_ref[bb, 2 * j + 1], 0.0), axis=1, keepdims=True)
                s_new = (s * row(d_ref) - jnp.where(lane_a, sk_a, sk_b) * row(b_ref)
                         + jnp.where(lane_a, v_a, v_b) * row(k_ref))
                st_ref[bb, j] = s_new
                o_a, o_b = pair_sum(s_new * row(r_ref))
                ot_ref[bb, 2 * j] = jnp.where(hot, o_a, ot_ref[bb, 2 * j])
                ot_ref[bb, 2 * j + 1] = jnp.where(hot, o_b, ot_ref[bb, 2 * j + 1])

    sub = 8
    if tt % sub:
        for t in range(tt):
            step(t, lambda ref, bb, ls: ref[bb, :, ls], t)
    else:
        def group(g, carry):
            base = pl.multiple_of(g * sub, sub)
            for u in range(sub):
                step(base + u, lambda ref, bb, ls: ref[bb, pl.ds(base, sub), ls], u)
            return carry

        lax.fori_loop(0, tt // sub, group, 0)


def rwkv_scan(r, d, k, kk, b, v_t, s0_packed, bb, tt):
    ns, t, w = r.shape
    row_spec = pl.BlockSpec((bb, tt, w), lambda n, i: (n, i, 0))
    vt_spec = pl.BlockSpec((bb, RWKV_H, HEAD_DIM, tt), lambda n, i: (n, 0, 0, i))
    st_spec = pl.BlockSpec((bb, RWKV_H // 2, HEAD_DIM, LANES), lambda n, i: (n, 0, 0, 0))
    return pl.pallas_call(
        functools.partial(_rwkv_scan_kernel, bb, tt),
        grid=(ns // bb, t // tt),
        in_specs=[row_spec] * 5 + [vt_spec, st_spec],
        out_specs=[vt_spec, st_spec],
        out_shape=[jax.ShapeDtypeStruct((ns, RWKV_H, HEAD_DIM, t), F32),
                   jax.ShapeDtypeStruct(s0_packed.shape, F32)],
        compiler_params=_cparams("parallel", "arbitrary"),
        name="rwkv_scan",
    )(r, d, k, kk, b, v_t, s0_packed)


def _pack_state(s):
    ns = s.shape[0]
    return (s.reshape(ns, RWKV_H // 2, 2, HEAD_DIM, HEAD_DIM).transpose(0, 1, 3, 2, 4)
            .reshape(ns, RWKV_H // 2, HEAD_DIM, LANES))


def _unpack_state(s):
    ns = s.shape[0]
    return (s.reshape(ns, RWKV_H // 2, HEAD_DIM, 2, HEAD_DIM).transpose(0, 1, 3, 2, 4)
            .reshape(ns, RWKV_H, HEAD_DIM, HEAD_DIM))


def _rwkv_readout(o, bonus, g, lnx_w, lnx_b):
    avg = _seg_ones(RWKV_W, HEAD_DIM, 1.0 / HEAD_DIM)
    mean = _mm_lhs2(o, avg)
    c = o - mean
    var = _mm_lhs2(c * c, avg)
    return ((c * lax.rsqrt(var + LNX_EPS)) * lnx_w + lnx_b + bonus) * g


def _proj_out_kernel(x_ref, o_ref, bonus_ref, g_ref, oa_ref, ln_ref, mod_ref, w_ref, y_ref):
    d = D_MODEL
    o_rwkv = _rwkv_readout(o_ref[...], bonus_ref[...], g_ref[...], ln_ref[0:1, :], ln_ref[1:2, :])
    mix = jnp.concatenate([o_rwkv, oa_ref[...]], axis=-1).astype(BF16)
    y = jnp.dot(mix, w_ref[...], preferred_element_type=F32)
    y_ref[...] = x_ref[...] + mod_ref[:, 2 * d:3 * d] * y


def proj_out(x, o_scan, bonus, g, o_attn, lnx, mod, w_out_bf16, tm):
    gg, r, d = x.shape
    tok = lambda width: pl.BlockSpec((None, tm, width), lambda g, i: (g, i, 0))
    full = lambda a: pl.BlockSpec(a.shape, lambda g, i: (0,) * a.ndim)
    return pl.pallas_call(
        _proj_out_kernel,
        grid=(gg, r // tm),
        in_specs=[tok(d), tok(RWKV_W), tok(RWKV_W), tok(RWKV_W), tok(ATTN_W), full(lnx),
                  _mod_spec(mod, tm), full(w_out_bf16)],
        out_specs=tok(d),
        out_shape=jax.ShapeDtypeStruct(x.shape, F32),
        compiler_params=_cparams("parallel", "parallel"),
        name="proj_out",
    )(x, o_scan, bonus, g, o_attn, lnx, mod, w_out_bf16)


def rope_tables(pos, width, group, rot_dim, rot_width):
    half = rot_dim // 2
    inv_freq = ROPE_THETA ** (-jnp.arange(half, dtype=F32) / half)
    ang = pos.astype(F32)[:, None] * inv_freq[None, :]
    lane = jnp.arange(width)
    j = lane % group
    rot = (lane < rot_width) & (j < rot_dim)
    cos = jnp.where(rot[None, :], jnp.cos(ang)[:, j % half], 1.0)
    sin = jnp.sin(ang)[:, j % half]
    sin_lo = jnp.where((rot & (j < half))[None, :], -sin, 0.0)
    sin_hi = jnp.where((rot & (j >= half))[None, :], sin, 0.0)
    return jnp.stack([cos, sin_lo, sin_hi]).astype(F32)


def _rope(x, tab_ref, half):
    width = x.shape[-1]
    return (x * tab_ref[0] + pltpu.roll(x, width - half, axis=1) * tab_ref[1]
            + pltpu.roll(x, half, axis=1) * tab_ref[2])


def _attn_prep_kernel(pa_ref, pi_ref, qkw_ref, ta_ref, ti_ref, q_ref, k_ref, qi_ref, ki_ref):
    avg = _seg_ones(ATTN_W, HEAD_DIM, 1.0 / HEAD_DIM)

    def head_norm(x, w):
        return x * lax.rsqrt(_mm_lhs2(x * x, avg) + NORM_EPS) * w

    q = head_norm(pa_ref[:, 0:ATTN_W], qkw_ref[0:1, :])
    k = head_norm(pa_ref[:, ATTN_W:2 * ATTN_W], qkw_ref[1:2, :])
    q_ref[...] = _rope(q, ta_ref, ROT_DIM // 2)
    k_ref[...] = _rope(k, ta_ref, ROT_DIM // 2)
    pi = _rope(pi_ref[...], ti_ref, IDX_ROT_DIM // 2)
    qi_ref[...] = pi[:, 0:IDX_QW]
    ki_ref[...] = pi[:, IDX_QW:]


def attn_prep(pa, pi, qk_norm_w, tab_attn, tab_idx, tm):
    g, r, _ = pa.shape
    tok = lambda width: pl.BlockSpec((None, tm, width), lambda g, i: (g, i, 0))

    def tab_spec(t):
        if t.shape[1] == 1:
            return pl.BlockSpec((3, 1, t.shape[2]), lambda g, i: (0, 0, 0))
        return pl.BlockSpec((3, tm, t.shape[2]), lambda g, i: (0, i, 0))

    widths = (ATTN_W, ATTN_W, IDX_QW, IDX_PAD_W - IDX_QW)
    return pl.pallas_call(
        _attn_prep_kernel,
        grid=(g, r // tm),
        in_specs=[tok(ATTN_IN_W), tok(IDX_PAD_W), pl.BlockSpec(qk_norm_w.shape, lambda g, i: (0, 0)),
                  tab_spec(tab_attn), tab_spec(tab_idx)],
        out_specs=[tok(w) for w in widths],
        out_shape=[jax.ShapeDtypeStruct((g, r, w), F32) for w in widths],
        compiler_params=_cparams("parallel", "parallel"),
        name="attn_prep",
    )(pa, pi, qk_norm_w, tab_attn, tab_idx)


def _sortable_key(score):
    bits = pltpu.bitcast(score + 0.0, I32)
    return bits ^ ((bits >> 31) & 0x7FFFFFFF)


def _kth_largest_key(count_ge, k_sel, shape):
    t = jnp.where(count_ge(jnp.zeros(shape, I32)) >= k_sel, 0, INT_MIN).astype(I32)

    def body(i, t):
        cand = t | jnp.left_shift(jnp.int32(1), 30 - i)
        return jnp.where(count_ge(cand) >= k_sel, cand, t)

    return lax.fori_loop(0, 31, body, t)


def _tie_index_limit(count_eq_below, need, n_bits, shape):
    def body(i, j):
        cand = j | jnp.left_shift(jnp.int32(1), n_bits - 1 - i)
        return jnp.where(count_eq_below(cand) < need, cand, j)

    return lax.fori_loop(0, n_bits, body, jnp.zeros(shape, I32))


def _dsa_prompt_kernel(tq, tk, k_sel, idx_bits, kidx_ref, k_ref, vt_ref, qit_ref, wt_ref, qt_ref, o_ref,
                       ikey_ref, thr_ref, m_ref, l_ref, acc_ref):
    qb = pl.program_id(1)
    n_chunks = ((qb + 1) * tq + tk - 1) // tk
    qpos = qb * tq + lax.broadcasted_iota(I32, (tk, tq), 1)
    row = lax.broadcasted_iota(I32, (tk, tq), 0)
    w = wt_ref[...] * (IDX_QW ** -0.5)

    def score_chunk(c, carry):
        rows = pl.ds(pl.multiple_of(c * tk, tk), tk)
        dots = jnp.dot(kidx_ref[rows, :], qit_ref[...], preferred_element_type=F32)
        sc = jnp.zeros((tk, tq), F32)
        for h in range(IDX_H):
            sc = sc + jnp.maximum(dots[:, h * tq:(h + 1) * tq], 0.0) * w[h:h + 1, :]
        sc = jnp.where(c * tk + row <= qpos, sc, -jnp.inf)
        ikey_ref[rows, :] = _sortable_key(sc)
        return carry

    lax.fori_loop(0, n_chunks, score_chunk, 0)

    def count(pred):
        def body(c, acc):
            rows = pl.ds(pl.multiple_of(c * tk, tk), tk)
            hit = pred(ikey_ref[rows, :], c * tk + row)
            return acc + jnp.sum(jnp.where(hit, 1.0, 0.0), axis=0, keepdims=True)
        return lax.fori_loop(0, n_chunks, body, jnp.zeros((1, tq), F32))

    thr = _kth_largest_key(lambda cand: count(lambda key, pos: key >= cand), k_sel, (1, tq))
    need = k_sel - count(lambda key, pos: key > thr)
    n_eq = count(lambda key, pos: key == thr)
    thr_ref[0:1, :] = thr
    thr_ref[1:2, :] = jnp.full((1, tq), 2 ** 30, I32)

    @pl.when(jnp.max(n_eq - need) > 0.0)
    def _():
        thr_ref[1:2, :] = _tie_index_limit(
            lambda cand: count(lambda key, pos: (key == thr) & (pos < cand)), need, idx_bits, (1, tq))

    m_ref[...] = jnp.full(m_ref.shape, -jnp.inf, F32)
    l_ref[...] = jnp.zeros(l_ref.shape, F32)
    acc_ref[...] = jnp.zeros(acc_ref.shape, F32)
    tie_j = thr_ref[1:2, :]

    def attend_chunk(c, carry):
        rows = pl.ds(pl.multiple_of(c * tk, tk), tk)
        key = ikey_ref[rows, :]
        pos = c * tk + row
        sel = ((key > thr) | ((key == thr) & (pos <= tie_j))) & (pos <= qpos)
        bias = jnp.where(sel, 0.0, -jnp.inf)
        for h in range(ATTN_H):
            pair = slice((h // 2) * LANES, (h // 2 + 1) * LANES)
            s = jnp.dot(k_ref[rows, pair], qt_ref[h], preferred_element_type=F32) * (HEAD_DIM ** -0.5) + bias
            m_old = m_ref[h:h + 1, :]
            m_new = jnp.maximum(m_old, jnp.max(s, axis=0, keepdims=True))
            m_safe = jnp.where(m_new == -jnp.inf, 0.0, m_new)
            alpha = jnp.exp(m_old - m_safe)
            p = jnp.exp(s - m_safe)
            hs = slice(h * HEAD_DIM, (h + 1) * HEAD_DIM)
            l_ref[h:h + 1, :] = alpha * l_ref[h:h + 1, :] + jnp.sum(p, axis=0, keepdims=True)
            acc_ref[hs, :] = alpha * acc_ref[hs, :] + jnp.dot(
                vt_ref[c, hs, :], p.astype(BF16), preferred_element_type=F32)
            m_ref[h:h + 1, :] = m_new
        return carry

    lax.fori_loop(0, n_chunks, attend_chunk, 0)
    for h in range(ATTN_H):
        hs = slice(h * HEAD_DIM, (h + 1) * HEAD_DIM)
        acc_ref[hs, :] = acc_ref[hs, :] / l_ref[h:h + 1, :]
    o_ref[...] = acc_ref[...].T


def dsa_prompt(q, k, v, qidx, kidx, widx, tq, tk):
    b, t, _ = q.shape
    nqb, nck = t // tq, t // tk
    k_sel = min(TOPK_MAX, t // 4)
    eye2 = jnp.eye(2, dtype=F32)
    q_t = q.reshape(b, nqb, tq, ATTN_H // 2, 2, HEAD_DIM).transpose(0, 1, 3, 4, 5, 2)
    q_t = (q_t[:, :, :, :, None] * eye2[None, None, None, :, :, None, None]).reshape(
        b, nqb, ATTN_H, LANES, tq).astype(BF16)
    v_t = v.reshape(b, nck, tk, ATTN_W).transpose(0, 1, 3, 2).astype(BF16)
    qi_t = qidx.reshape(b, nqb, tq, IDX_H, IDX_D).transpose(0, 1, 4, 3, 2).reshape(
        b, nqb, IDX_D, IDX_H * tq).astype(BF16)
    w_t = widx.reshape(b, nqb, tq, IDX_H).transpose(0, 1, 3, 2)
    once = dict(pipeline_mode=pl.Buffered(1))
    return pl.pallas_call(
        functools.partial(_dsa_prompt_kernel, tq, tk, k_sel, (t - 1).bit_length()),
        grid=(b, nqb),
        in_specs=[
            pl.BlockSpec((None, t, IDX_D), lambda b, i: (b, 0, 0), **once),
            pl.BlockSpec((None, t, ATTN_W), lambda b, i: (b, 0, 0), **once),
            pl.BlockSpec((None, nck, ATTN_W, tk), lambda b, i: (b, 0, 0, 0), **once),
            pl.BlockSpec((None, None, IDX_D, IDX_H * tq), lambda b, i: (b, i, 0, 0)),
            pl.BlockSpec((None, None, IDX_H, tq), lambda b, i: (b, i, 0, 0)),
            pl.BlockSpec((None, None, ATTN_H, LANES, tq), lambda b, i: (b, i, 0, 0, 0)),
        ],
        out_specs=pl.BlockSpec((None, tq, ATTN_W), lambda b, i: (b, i, 0)),
        out_shape=jax.ShapeDtypeStruct((b, t, ATTN_W), F32),
        scratch_shapes=[
            pltpu.VMEM((t, tq), I32),
            pltpu.VMEM((8, tq), I32),
            pltpu.VMEM((ATTN_H, tq), F32),
            pltpu.VMEM((ATTN_H, tq), F32),
            pltpu.VMEM((ATTN_W, tq), F32),
        ],
        compiler_params=_cparams("parallel", "arbitrary"),
        name="dsa_prompt",
    )(kidx.astype(BF16), k.astype(BF16), v_t, qi_t, w_t, q_t)


_NT = (((1,), (1,)), ((), ()))


def _sample_scores_kernel(pt_ref, page_ref, qi_ref, w_ref, o_ref):
    dots = lax.dot_general(qi_ref[...].astype(BF16), page_ref[...].astype(BF16), _NT,
                           preferred_element_type=F32)
    w = w_ref[...] * (IDX_QW ** -0.5)
    o_ref[...] = jnp.sum(jnp.maximum(dots, 0.0) * w, axis=0, keepdims=True)


def sample_scores(layer, page_table_flat, cache_kidx, qidx, widx, n_pages):
    nb = qidx.shape[0]
    return pl.pallas_call(
        _sample_scores_kernel,
        grid_spec=pltpu.PrefetchScalarGridSpec(
            num_scalar_prefetch=1,
            grid=(nb, n_pages),
            in_specs=[
                pl.BlockSpec((None, None, PAGE_SIZE, IDX_D),
                             lambda b, j, pt: (layer, pt[b * n_pages + j], 0, 0)),
                pl.BlockSpec((None, IDX_H, IDX_D), lambda b, j, pt: (b, 0, 0)),
                pl.BlockSpec((None, IDX_H, 1), lambda b, j, pt: (b, 0, 0)),
            ],
            out_specs=pl.BlockSpec((None, None, 1, PAGE_SIZE), lambda b, j, pt: (b, j, 0, 0)),
        ),
        out_shape=jax.ShapeDtypeStruct((nb, n_pages, 1, PAGE_SIZE), F32),
        compiler_params=_cparams("parallel", "arbitrary"),
        name="sample_scores",
    )(page_table_flat, cache_kidx, qidx, widx)


def _sample_select_kernel(k_sel, idx_bits, sc_ref, qi_ref, kt_ref, w_ref, bias_ref, bias_self_ref):
    nb, nk = sc_ref.shape
    r = lax.broadcasted_iota(I32, (IDX_QW, LANES), 0) // IDX_D
    c = lax.broadcasted_iota(I32, (IDX_QW, LANES), 1)
    seg = jnp.where(r == c, 1.0, 0.0).astype(BF16)
    prod = qi_ref[...].astype(BF16).astype(F32) * kt_ref[...].astype(BF16).astype(F32)
    dots = _mm_lhs2(prod, seg)
    self_sc = jnp.sum(jnp.maximum(dots, 0.0) * (w_ref[...] * (IDX_QW ** -0.5)), axis=1, keepdims=True)
    key_self = _sortable_key(self_sc)
    keys = _sortable_key(sc_ref[...])
    pos = lax.broadcasted_iota(I32, (nb, nk), 1)

    def count(pred):
        past = jnp.sum(jnp.where(pred(keys, pos), 1.0, 0.0), axis=1, keepdims=True)
        return past + jnp.where(pred(key_self, nk), 1.0, 0.0)

    thr = _kth_largest_key(lambda cand: count(lambda key, p: key >= cand), k_sel, (nb, 1))
    need = k_sel - count(lambda key, p: key > thr)
    tie_j = _tie_index_limit(
        lambda cand: count(lambda key, p: (key == thr) & (p < cand)), need, idx_bits, (nb, 1))
    chosen = lambda key, p: (key > thr) | ((key == thr) & (p <= tie_j))
    bias_ref[...] = jnp.where(chosen(keys, pos), 0.0, -jnp.inf)
    bias_self_ref[...] = jnp.broadcast_to(jnp.where(chosen(key_self, nk), 0.0, -jnp.inf), (nb, LANES))


def sample_select(scores, qidx, kidx_tiled, w_pad):
    nb, nk = scores.shape
    k_sel = min(TOPK_MAX, (nk + 1) // 4)
    return pl.pallas_call(
        functools.partial(_sample_select_kernel, k_sel, nk.bit_length()),
        out_shape=[jax.ShapeDtypeStruct((nb, nk), F32), jax.ShapeDtypeStruct((nb, LANES), F32)],
        compiler_params=pltpu.CompilerParams(vmem_limit_bytes=VMEM_LIMIT),
        name="sample_select",
    )(scores, qidx, kidx_tiled, w_pad)


def _sample_attend_kernel(pt_ref, kp_ref, vp_ref, bias_ref, q_ref, kn_ref, vn_ref, bself_ref, o_ref,
                          m_ref, l_ref, acc_ref, s_ref, pv_ref):
    j = pl.program_id(1)

    @pl.when(j == 0)
    def _():
        m_ref[...] = jnp.full(m_ref.shape, -jnp.inf, F32)
        l_ref[...] = jnp.zeros(l_ref.shape, F32)
        acc_ref[...] = jnp.zeros(acc_ref.shape, F32)

    def online_update(s, pv_of):
        m_old = m_ref[...]
        m_new = jnp.maximum(m_old, jnp.max(s, axis=1, keepdims=True))
        m_safe = jnp.where(m_new == -jnp.inf, 0.0, m_new)
        alpha = jnp.exp(m_old - m_safe)
        p = jnp.exp(s - m_safe)
        l_ref[...] = alpha * l_ref[...] + jnp.sum(p, axis=1, keepdims=True)
        acc_ref[...] = alpha * acc_ref[...] + pv_of(p)
        m_ref[...] = m_new

    q = q_ref[...]
    for h in range(ATTN_H):
        s_ref[h:h + 1, :] = lax.dot_general(q[h:h + 1, :].astype(BF16), kp_ref[:, h, :].astype(BF16), _NT,
                                            preferred_element_type=F32)

    def paged_pv(p):
        for h in range(ATTN_H):
            pv_ref[h:h + 1, :] = jnp.dot(p[h:h + 1, :].astype(BF16), vp_ref[:, h, :].astype(BF16),
                                         preferred_element_type=F32)
        return pv_ref[...]

    online_update(s_ref[...] * (HEAD_DIM ** -0.5) + bias_ref[...], paged_pv)

    @pl.when(j == pl.num_programs(1) - 1)
    def _():
        s_self = jnp.sum(q * kn_ref[...], axis=1, keepdims=True) * (HEAD_DIM ** -0.5) + bself_ref[:, 0:1]
        online_update(s_self, lambda p: p * vn_ref[...])
        o_ref[...] = acc_ref[...] / l_ref[...]


def sample_attend(layer, page_table_flat, cache_k, cache_v, bias, bias_self, q, k_new, v_new, n_pages):
    nb = q.shape[0]
    page = pl.BlockSpec((None, None, PAGE_SIZE, ATTN_H, HEAD_DIM),
                        lambda b, j, pt: (layer, pt[b * n_pages + j], 0, 0, 0))
    row = pl.BlockSpec((None, ATTN_H, HEAD_DIM), lambda b, j, pt: (b, 0, 0))
    return pl.pallas_call(
        _sample_attend_kernel,
        grid_spec=pltpu.PrefetchScalarGridSpec(
            num_scalar_prefetch=1,
            grid=(nb, n_pages),
            in_specs=[page, page,
                      pl.BlockSpec((None, None, 1, PAGE_SIZE), lambda b, j, pt: (b, j, 0, 0)),
                      row, row, row,
                      pl.BlockSpec((None, 1, LANES), lambda b, j, pt: (b, 0, 0))],
            out_specs=row,
            scratch_shapes=[pltpu.VMEM((ATTN_H, 1), F32), pltpu.VMEM((ATTN_H, 1), F32),
                            pltpu.VMEM((ATTN_H, HEAD_DIM), F32), pltpu.VMEM((ATTN_H, PAGE_SIZE), F32),
                            pltpu.VMEM((ATTN_H, HEAD_DIM), F32)],
        ),
        out_shape=jax.ShapeDtypeStruct((nb, ATTN_H, HEAD_DIM), F32),
        compiler_params=_cparams("parallel", "arbitrary"),
        name="sample_attend",
    )(page_table_flat, cache_k, cache_v, bias, q, k_new, v_new, bias_self)


def _ffn_kernel(routed, x_ref, nw_ref, mod_ref, router_ref, wg_ref, wu_ref, wd_ref, y_ref,
                h_ref, acc_ref, comb_ref):
    d = D_MODEL
    j = pl.program_id(2)

    @pl.when(j == 0)
    def _():
        h = _modulated_norm(x_ref[...], nw_ref[...], mod_ref[:, 3 * d:4 * d], mod_ref[:, 4 * d:5 * d])
        h_ref[...] = h.astype(BF16)
        acc_ref[...] = jnp.zeros(acc_ref.shape, F32)
        if routed:
            lane = lax.broadcasted_iota(I32, comb_ref.shape, 1).astype(F32)
            logits = jnp.where(lane < N_EXPERTS, _mm3(h, router_ref[...]), -jnp.inf)
            m1 = jnp.max(logits, axis=1, keepdims=True)
            i1 = jnp.min(jnp.where(logits == m1, lane, float(LANES)), axis=1, keepdims=True)
            rest = jnp.where(lane == i1, -jnp.inf, logits)
            m2 = jnp.max(rest, axis=1, keepdims=True)
            i2 = jnp.min(jnp.where(rest == m2, lane, float(LANES)), axis=1, keepdims=True)
            e2 = jnp.exp(m2 - m1)
            comb_ref[...] = (jnp.where(lane == i1, 1.0 / (1.0 + e2), 0.0)
                             + jnp.where(lane == i2, e2 / (1.0 + e2), 0.0))

    h = h_ref[...]
    t = _silu(jnp.dot(h, wg_ref[...], preferred_element_type=F32)) * jnp.dot(
        h, wu_ref[...], preferred_element_type=F32)
    y = jnp.dot(t.astype(BF16), wd_ref[...], preferred_element_type=F32)
    if routed:
        lane = lax.broadcasted_iota(I32, comb_ref.shape, 1)
        y = y * jnp.sum(jnp.where(lane == j, comb_ref[...], 0.0), axis=1, keepdims=True)
    acc_ref[...] += y

    @pl.when(j == pl.num_programs(2) - 1)
    def _():
        y_ref[...] = x_ref[...] + mod_ref[:, 5 * d:6 * d] * acc_ref[...]


def ffn(x, norm_w, mod, router_pad, wg, wu, wd, tm, routed):
    g, r, d = x.shape
    nj, _, f = wg.shape
    tok = pl.BlockSpec((None, tm, d), lambda g, i, j: (g, i, 0))
    if mod.shape[1] == 1:
        mod_spec = pl.BlockSpec((None, 1, mod.shape[2]), lambda g, i, j: (g, 0, 0))
    else:
        mod_spec = pl.BlockSpec((None, tm, mod.shape[2]), lambda g, i, j: (g, i, 0))
    return pl.pallas_call(
        functools.partial(_ffn_kernel, routed),
        grid=(g, r // tm, nj),
        in_specs=[tok, pl.BlockSpec((1, d), lambda g, i, j: (0, 0)), mod_spec,
                  pl.BlockSpec(router_pad.shape, lambda g, i, j: (0, 0)),
                  pl.BlockSpec((None, d, f), lambda g, i, j: (j, 0, 0)),
                  pl.BlockSpec((None, d, f), lambda g, i, j: (j, 0, 0)),
                  pl.BlockSpec((None, f, d), lambda g, i, j: (j, 0, 0))],
        out_specs=tok,
        out_shape=jax.ShapeDtypeStruct(x.shape, F32),
        scratch_shapes=[pltpu.VMEM((tm, d), BF16), pltpu.VMEM((tm, d), F32), pltpu.VMEM((tm, LANES), F32)],
        compiler_params=_cparams("parallel", "parallel", "arbitrary"),
        name="ffn_routed" if routed else "ffn_dense",
    )(x, norm_w.reshape(1, d), mod, router_pad, wg, wu, wd)


def _layer_params(l, p):
    z = jnp.zeros((LORA_W, RWKV_W), F32)
    w_in = jnp.pad(p["w_in"][l], ((0, 0), (0, SHIFT_W + ATTN_IN_W + IDX_PAD_W - IN_W))).astype(BF16)
    j = l // 2
    if l % 2 == 0:
        halves = lambda w: jnp.stack(jnp.split(w, w.shape[1] // D_FF_TILE, axis=1)).astype(BF16)
        wg, wu = halves(p["ffn_w_gate"][j]), halves(p["ffn_w_up"][j])
        wd = jnp.stack(jnp.split(p["ffn_w_down"][j], wg.shape[0], axis=0)).astype(BF16)
        router = jnp.zeros((8, LANES), F32)
    else:
        wg, wu, wd = (p[n][j].astype(BF16) for n in ("exp_w_gate", "exp_w_up", "exp_w_down"))
        router = jnp.pad(p["router_w"][j], ((0, 0), (0, LANES - N_EXPERTS)))
    return dict(
        norm1_w=p["norm1_w"][l], norm2_w=p["norm2_w"][l], w_in=w_in, mu=p["mu_shift"][l][None, :],
        vecs=jnp.stack([p["rwkv_w0"][l], p["rwkv_a0"][l], p["rwkv_k_k"][l], p["rwkv_k_a"][l],
                        p["rwkv_r_k"][l].reshape(RWKV_W)] + [jnp.zeros((RWKV_W,), F32)] * 3),
        w2=z.at[0:W_LORA].set(p["rwkv_w2"][l]),
        a2=z.at[W_LORA:W_LORA + A_LORA].set(p["rwkv_a2"][l]),
        g2=z.at[W_LORA + A_LORA:].set(p["rwkv_g2"][l]),
        lnx=jnp.stack([p["lnx_w"][l], p["lnx_b"][l]]),
        qkw=jnp.stack([jnp.tile(p["q_norm_w"][l], ATTN_H), jnp.tile(p["k_norm_w"][l], ATTN_H)]),
        w_out=p["w_out"][l].astype(BF16), router=router, wg=wg, wu=wu, wd=wd, routed=(l % 2 == 1))


def _trunk_layer(x, mod, lp, shift_prev, s0, tabs, seq_mode, tm, attend):
    g, r, _ = x.shape
    pr, pa, pi = proj_in(x, lp["norm1_w"], mod, lp["w_in"], tm)
    rr, dd, kr, vr, kk, bb_, gate, bonus = rwkv_prep(pr, shift_prev, lp["mu"], lp["vecs"], lp["w2"], lp["a2"],
                                                     lp["g2"], tm, seq_mode)
    if seq_mode:
        seqs = lambda a: a
        v_t = vr.reshape(g, r, RWKV_H, HEAD_DIM).transpose(0, 2, 3, 1)
        o_t, s_fin = rwkv_scan(rr, dd, kr, kk, bb_, v_t, _pack_state(s0), g, LANES)
        o_scan = o_t.transpose(0, 3, 1, 2).reshape(g, r, RWKV_W)
    else:
        seqs = lambda a: a.reshape(r, 1, RWKV_W)
        o_t, s_fin = rwkv_scan(seqs(rr), seqs(dd), seqs(kr), seqs(kk), seqs(bb_),
                               vr.reshape(r, RWKV_H, HEAD_DIM, 1), _pack_state(s0), 8, 1)
        o_scan = o_t.reshape(g, r, RWKV_W)
    q, k, qi, ki = attn_prep(pa, pi, lp["qkw"], tabs[0], tabs[1], tm)
    v = pa[..., 2 * ATTN_W:]
    kidx = ki[..., :IDX_D]
    widx = ki[..., IDX_D:IDX_D + IDX_H]
    o_attn = attend(q, k, v, qi, kidx, widx)
    x = proj_out(x, o_scan, bonus, gate, o_attn, lp["lnx"], mod, lp["w_out"], tm)
    x = ffn(x, lp["norm2_w"], mod, lp["router"], lp["wg"], lp["wu"], lp["wd"], min(512, r), lp["routed"])
    return x, (k, v, kidx, _unpack_state(s_fin), pr[:, -1] if seq_mode else pr[0])


def kernel(x_prompt, x_sample, c_prompt, c_sample, cache_k, cache_v, cache_kidx, state_wkv, state_shift, page_table, norm1_w, norm2_w, w_ada, b_ada, w_in, mu_shift, rwkv_w0, rwkv_w2, rwkv_a0, rwkv_a2, rwkv_g2, rwkv_k_k, rwkv_k_a, rwkv_r_k, lnx_w, lnx_b, q_norm_w, k_norm_w, w_out, ffn_w_gate, ffn_w_up, ffn_w_down, router_w, exp_w_gate, exp_w_up, exp_w_down):
    params = dict(norm1_w=norm1_w, norm2_w=norm2_w, w_in=w_in, mu_shift=mu_shift, rwkv_w0=rwkv_w0,
                  rwkv_w2=rwkv_w2, rwkv_a0=rwkv_a0, rwkv_a2=rwkv_a2, rwkv_g2=rwkv_g2, rwkv_k_k=rwkv_k_k,
                  rwkv_k_a=rwkv_k_a, rwkv_r_k=rwkv_r_k, lnx_w=lnx_w, lnx_b=lnx_b, q_norm_w=q_norm_w,
                  k_norm_w=k_norm_w, w_out=w_out, ffn_w_gate=ffn_w_gate, ffn_w_up=ffn_w_up,
                  ffn_w_down=ffn_w_down, router_w=router_w, exp_w_gate=exp_w_gate, exp_w_up=exp_w_up,
                  exp_w_down=exp_w_down)
    depth = w_in.shape[0]
    bp, t, d = x_prompt.shape
    nb = x_sample.shape[0]
    n_pages = page_table.shape[1]
    past = n_pages * PAGE_SIZE
    pt_flat = page_table.reshape(-1).astype(I32)

    n_mod = bp + nb
    c_all = jnp.pad(jnp.concatenate([c_prompt, c_sample]), ((0, -n_mod % 8), (0, 0)))
    mod = ada_modulation(c_all, w_ada, b_ada)
    mod_p = mod[:, :bp, None, :]
    mod_s = mod[:, None, bp:n_mod, :]

    def tables(pos):
        return (rope_tables(pos, ATTN_W, HEAD_DIM, ROT_DIM, ATTN_W),
                rope_tables(pos, IDX_PAD_W, IDX_D, IDX_ROT_DIM, IDX_QW + IDX_D))

    tabs_p = tables(jnp.arange(t, dtype=I32))
    tabs_s = tables(jnp.full((1,), past, I32))
    tm_p = min(256, t)
    tq, tk = min(128, t), min(512, t)

    xp, xs = x_prompt, x_sample.reshape(1, nb, d)
    st_p, st_s = [], []
    for l in range(depth):
        lp = _layer_params(l, params)
        attend_p = lambda q, k, v, qi, ki, wi: dsa_prompt(q, k, v, qi, ki, wi, tq, tk)

        def attend_s(q, k, v, qi, ki, wi, l=l):
            heads = lambda a: a.reshape(nb, ATTN_H, HEAD_DIM)
            qi_h = qi.reshape(nb, IDX_H, IDX_D)
            scores = sample_scores(l, pt_flat, cache_kidx, qi_h, wi.reshape(nb, IDX_H, 1), n_pages)
            w_pad = jnp.pad(wi.reshape(nb, IDX_H), ((0, 0), (0, LANES - IDX_H)))
            bias, bias_self = sample_select(scores.reshape(nb, past), qi.reshape(nb, IDX_QW),
                                            jnp.tile(ki.reshape(nb, IDX_D), (1, IDX_H)), w_pad)
            o = sample_attend(l, pt_flat, cache_k, cache_v, bias.reshape(nb, n_pages, 1, PAGE_SIZE),
                              bias_self.reshape(nb, 1, LANES), heads(q), heads(k), heads(v), n_pages)
            return o.reshape(1, nb, ATTN_W)

        xp, sp = _trunk_layer(xp, mod_p[l], lp, jnp.zeros((bp, 1, SHIFT_W), F32),
                              jnp.zeros((bp, RWKV_H, HEAD_DIM, HEAD_DIM), F32), tabs_p, True, tm_p, attend_p)
        xs, ss = _trunk_layer(xs, mod_s[l], lp, state_shift[l][None], state_wkv[l], tabs_s, False, nb, attend_s)
        st_p.append(sp)
        st_s.append(ss)

    def stack(states, i, shape):
        return jnp.stack([s[i] for s in states]).reshape(shape)

    return (xp, xs.reshape(nb, 1, d),
            stack(st_p, 0, (depth, bp, t, ATTN_H, HEAD_DIM)), stack(st_p, 1, (depth, bp, t, ATTN_H, HEAD_DIM)),
            stack(st_p, 2, (depth, bp, t, IDX_D)), stack(st_p, 3, (depth, bp, RWKV_H, HEAD_DIM, HEAD_DIM)),
            stack(st_p, 4, (depth, bp, SHIFT_W)),
            stack(st_s, 0, (depth, nb, 1, ATTN_H, HEAD_DIM)), stack(st_s, 1, (depth, nb, 1, ATTN_H, HEAD_DIM)),
            stack(st_s, 2, (depth, nb, 1, IDX_D)), stack(st_s, 3, (depth, nb, RWKV_H, HEAD_DIM, HEAD_DIM)),
            stack(st_s, 4, (depth, nb, SHIFT_W)))
```

```python
import functools

import jax
import jax.numpy as jnp
from jax import lax
from jax.experimental import pallas as pl
from jax.experimental.pallas import tpu as pltpu

F32 = jnp.float32
BF16 = jnp.bfloat16
I32 = jnp.int32

D_MODEL = 1024
HEAD_DIM = 64
RWKV_W = 512
RWKV_H = 8
ATTN_W = 512
ATTN_H = 8
W_LORA = 64
A_LORA = 64
G_LORA = 128
LORA_W = W_LORA + A_LORA + G_LORA
IDX_H = 8
IDX_D = 32
IDX_QW = IDX_H * IDX_D
TOPK_MAX = 256
ROT_DIM = 16
IDX_ROT_DIM = 8
ROPE_THETA = 500000.0
PAGE_SIZE = 128
D_FF_TILE = 1408
N_EXPERTS = 8
NORM_EPS = 1e-6
LNX_EPS = 64e-5
SHIFT_W = 3 * RWKV_W + LORA_W
ATTN_IN_W = ATTN_W + 2 * ATTN_W
IDX_IN_W = IDX_QW + IDX_D + IDX_H
IDX_PAD_W = 384
IN_W = SHIFT_W + ATTN_IN_W + IDX_IN_W

LANES = 128
VMEM_LIMIT = 56 * 1024 * 1024
INT_MIN = -(2 ** 31)


def _cparams(*sem):
    return pltpu.CompilerParams(dimension_semantics=sem, vmem_limit_bytes=VMEM_LIMIT)


def _mm(a, b):
    return jnp.dot(a.astype(BF16), b.astype(BF16), preferred_element_type=F32)


def _split2(a):
    hi = a.astype(BF16)
    lo = (a - hi.astype(F32)).astype(BF16)
    return hi, lo


def _mm_lhs2(a, b_bf16):
    hi, lo = _split2(a)
    return (jnp.dot(hi, b_bf16, preferred_element_type=F32)
            + jnp.dot(lo, b_bf16, preferred_element_type=F32))


def _mm3(a, b):
    a_hi, a_lo = _split2(a)
    b_hi, b_lo = _split2(b)
    return (jnp.dot(a_hi, b_hi, preferred_element_type=F32)
            + jnp.dot(a_lo, b_hi, preferred_element_type=F32)
            + jnp.dot(a_hi, b_lo, preferred_element_type=F32))


def _silu(x):
    return x * jax.nn.sigmoid(x)


def _seg_ones(width, seg, scale=1.0):
    r = lax.broadcasted_iota(I32, (width, width), 0) // seg
    c = lax.broadcasted_iota(I32, (width, width), 1) // seg
    return jnp.where(r == c, scale, 0.0).astype(BF16)


def _ada_kernel(c_ref, w_ref, b_ref, o_ref):
    o_ref[...] = _mm3(_silu(c_ref[...]), w_ref[...]) + b_ref[...]


def ada_modulation(c_all, w_ada, b_ada):
    m, d = c_all.shape
    nl, _, n = w_ada.shape
    tn = 1536
    return pl.pallas_call(
        _ada_kernel,
        grid=(nl, n // tn),
        in_specs=[
            pl.BlockSpec((m, d), lambda l, j: (0, 0)),
            pl.BlockSpec((None, d, tn), lambda l, j: (l, 0, j)),
            pl.BlockSpec((None, 1, tn), lambda l, j: (l, 0, j)),
        ],
        out_specs=pl.BlockSpec((None, m, tn), lambda l, j: (l, 0, j)),
        out_shape=jax.ShapeDtypeStruct((nl, m, n), F32),
        compiler_params=_cparams("parallel", "parallel"),
        name="ada_modulation",
    )(c_all, w_ada, b_ada.reshape(nl, 1, n))


def _modulated_norm(x, norm_w, shift, scale):
    y = x * lax.rsqrt(jnp.mean(x * x, axis=-1, keepdims=True) + NORM_EPS)
    return (y * norm_w) * (1.0 + scale) + shift


def _proj_in_kernel(x_ref, nw_ref, mod_ref, w_ref, pr_ref, pa_ref, pi_ref):
    d = D_MODEL
    h = _modulated_norm(x_ref[...], nw_ref[...], mod_ref[:, 0:d], mod_ref[:, d:2 * d])
    p = jnp.dot(h.astype(BF16), w_ref[...], preferred_element_type=F32)
    pr_ref[...] = p[:, :SHIFT_W]
    pa_ref[...] = p[:, SHIFT_W:SHIFT_W + ATTN_IN_W]
    pi_ref[...] = p[:, SHIFT_W + ATTN_IN_W:]


def _mod_spec(mod, tm):
    if mod.shape[1] == 1:
        return pl.BlockSpec((None, 1, mod.shape[2]), lambda g, i: (g, 0, 0))
    return pl.BlockSpec((None, tm, mod.shape[2]), lambda g, i: (g, i, 0))


def proj_in(x, norm_w, mod, w_in_bf16, tm):
    g, r, d = x.shape
    n = w_in_bf16.shape[1]
    widths = (SHIFT_W, ATTN_IN_W, n - SHIFT_W - ATTN_IN_W)
    return pl.pallas_call(
        _proj_in_kernel,
        grid=(g, r // tm),
        in_specs=[
            pl.BlockSpec((None, tm, d), lambda g, i: (g, i, 0)),
            pl.BlockSpec((1, d), lambda g, i: (0, 0)),
            _mod_spec(mod, tm),
            pl.BlockSpec((d, n), lambda g, i: (0, 0)),
        ],
        out_specs=[pl.BlockSpec((None, tm, w), lambda g, i: (g, i, 0)) for w in widths],
        out_shape=[jax.ShapeDtypeStruct((g, r, w), F32) for w in widths],
        compiler_params=_cparams("parallel", "parallel"),
        name="proj_in",
    )(x, norm_w.reshape(1, d), mod, w_in_bf16)


def _rwkv_prep_kernel(seq_mode, p_ref, halo_ref, sp_ref, mu_ref, vec_ref, w2_ref, a2_ref, g2_ref,
                      r_ref, d_ref, k_ref, v_ref, kk_ref, b_ref, g_ref, bonus_ref):
    p = p_ref[...]
    if seq_mode:
        first = jnp.where(pl.program_id(1) == 0, sp_ref[...], halo_ref[7:8, :])
        row = lax.broadcasted_iota(I32, p.shape, 0)
        prev = jnp.where(row == 0, first, pltpu.roll(p, 1, axis=0))
    else:
        prev = sp_ref[...]
    xs = p + mu_ref[...] * (prev - p)
    w = RWKV_W
    r = xs[:, 0:w]
    k = xs[:, w:2 * w]
    v = xs[:, 2 * w:3 * w]
    lo = xs[:, 3 * w:]
    lane = lax.broadcasted_iota(I32, lo.shape, 1)
    act = jnp.where(lane < W_LORA, jnp.tanh(lo),
                    jnp.where(lane < W_LORA + A_LORA, lo, jax.nn.sigmoid(lo)))
    w0, a0, k_k, k_a, r_k = (vec_ref[i:i + 1, :] for i in range(5))
    z = -(w0 + _mm3(act, w2_ref[...]))
    softplus = jnp.maximum(z, 0.0) + jnp.log(1.0 + jnp.exp(-jnp.abs(z)))
    decay = jnp.exp(-jnp.exp(-softplus - 0.5))
    a = jax.nn.sigmoid(a0 + _mm3(act, a2_ref[...]))
    g = _mm3(act, g2_ref[...])
    seg = _seg_ones(w, HEAD_DIM)
    kk = k * k_k
    kk = kk * lax.rsqrt(jnp.maximum(_mm_lhs2(kk * kk, seg), 1e-24))
    k2 = k * (1.0 + (a - 1.0) * k_a)
    r_ref[...] = r
    d_ref[...] = decay
    k_ref[...] = k2
    v_ref[...] = v
    kk_ref[...] = kk
    b_ref[...] = kk * a
    g_ref[...] = g
    bonus_ref[...] = _mm_lhs2(r * k2 * r_k, seg) * v


def rwkv_prep(p, shift_prev, mu, vecs, w2p, a2p, g2p, tm, seq_mode):
    g, r, sw = p.shape
    w = RWKV_W
    tok = lambda width: pl.BlockSpec((None, tm, width), lambda g, i: (g, i, 0))
    full = lambda a: pl.BlockSpec(a.shape, lambda g, i: (0,) * a.ndim)
    if seq_mode:
        halo_spec = pl.BlockSpec((None, 8, sw), lambda g, i: (g, jnp.maximum(i * (tm // 8) - 1, 0), 0))
        sp_spec = pl.BlockSpec((None, 1, sw), lambda g, i: (g, 0, 0))
    else:
        halo_spec = pl.BlockSpec((None, 8, sw), lambda g, i: (g, 0, 0))
        sp_spec = tok(sw)
    return pl.pallas_call(
        functools.partial(_rwkv_prep_kernel, seq_mode),
        grid=(g, r // tm),
        in_specs=[tok(sw), halo_spec, sp_spec, full(mu), full(vecs), full(w2p), full(a2p), full(g2p)],
        out_specs=[tok(w)] * 8,
        out_shape=[jax.ShapeDtypeStruct((g, r, w), F32)] * 8,
        compiler_params=_cparams("parallel", "arbitrary"),
        name="rwkv_prep",
    )(p, p, shift_prev, mu, vecs, w2p, a2p, g2p)


def _rwkv_scan_kernel(bb_n, tt, r_ref, d_ref, k_ref, kk_ref, b_ref, vt_ref, s0_ref, ot_ref, st_ref):
    @pl.when(pl.program_id(1) == 0)
    def _():
        st_ref[...] = s0_ref[...]

    ot_ref[...] = jnp.zeros_like(ot_ref)
    lane_a = lax.broadcasted_iota(I32, (HEAD_DIM, LANES), 1) < HEAD_DIM
    t_lane = lax.broadcasted_iota(I32, (HEAD_DIM, tt), 1)

    def pair_sum(x):
        s_a = jnp.sum(jnp.where(lane_a, x, 0.0), axis=1, keepdims=True)
        s_b = jnp.sum(jnp.where(lane_a, 0.0, x), axis=1, keepdims=True)
        return s_a, s_b

    pairs = [(bb, j) for bb in range(bb_n) for j in range(RWKV_H // 2)]
    half_ones = _seg_ones(LANES, HEAD_DIM)

    def step(t, rows_of, u):
        hot = t_lane == t

        def rows_all(ref):
            return jnp.concatenate(
                [jnp.broadcast_to(rows_of(ref, bb, slice(j * LANES, (j + 1) * LANES))[u:u + 1, :],
                                  (HEAD_DIM, LANES)) for bb, j in pairs], axis=0)

        def value_cols(bb, j):
            v_a = jnp.sum(jnp.where(hot, vt_ref[bb, 2 * j], 0.0), axis=1, keepdims=True)
            v_b = jnp.sum(jnp.where(hot, vt_ref[bb, 2 * j + 1], 0.0), axis=1, keepdims=True)
            return jnp.where(lane_a, v_a, v_b)

        s = jnp.concatenate([st_ref[bb, j] for bb, j in pairs], axis=0)
        sk = _mm_lhs2(s * rows_all(kk_ref), half_ones)
        vv = jnp.concatenate([value_cols(bb, j) for bb, j in pairs], axis=0)
        s_new = s * rows_all(d_ref) - sk * rows_all(b_ref) + vv * rows_all(k_ref)
        q = s_new * rows_all(r_ref)
        for i, (bb, j) in enumerate(pairs):
            rows = slice(i * HEAD_DIM, (i + 1) * HEAD_DIM)
            st_ref[bb, j] = s_new[rows]
            o_a, o_b = pair_sum(q[rows])
            ot_ref[bb, 2 * j] = jnp.where(hot, o_a, ot_ref[bb, 2 * j])
            ot_ref[bb, 2 * j + 1] = jnp.where(hot, o_b, ot_ref[bb, 2 * j + 1])

    sub = 8
    if tt % sub:
        for t in range(tt):
            step(t, lambda ref, bb, ls: ref[bb, :, ls], t)
    else:
        def group(g, carry):
            base = pl.multiple_of(g * sub, sub)
            for u in range(sub):
                step(base + u, lambda ref, bb, ls: ref[bb, pl.ds(base, sub), ls], u)
            return carry

        lax.fori_loop(0, tt // sub, group, 0)


def rwkv_scan(r, d, k, kk, b, v_t, s0_packed, bb, tt):
    ns, t, w = r.shape
    row_spec = pl.BlockSpec((bb, tt, w), lambda n, i: (n, i, 0))
    vt_spec = pl.BlockSpec((bb, RWKV_H, HEAD_DIM, tt), lambda n, i: (n, 0, 0, i))
    st_spec = pl.BlockSpec((bb, RWKV_H // 2, HEAD_DIM, LANES), lambda n, i: (n, 0, 0, 0))
    return pl.pallas_call(
        functools.partial(_rwkv_scan_kernel, bb, tt),
        grid=(ns // bb, t // tt),
        in_specs=[row_spec] * 5 + [vt_spec, st_spec],
        out_specs=[vt_spec, st_spec],
        out_shape=[jax.ShapeDtypeStruct((ns, RWKV_H, HEAD_DIM, t), F32),
                   jax.ShapeDtypeStruct(s0_packed.shape, F32)],
        compiler_params=_cparams("parallel", "arbitrary"),
        name="rwkv_scan",
    )(r, d, k, kk, b, v_t, s0_packed)


def _pack_state(s):
    ns = s.shape[0]
    return (s.reshape(ns, RWKV_H // 2, 2, HEAD_DIM, HEAD_DIM).transpose(0, 1, 3, 2, 4)
            .reshape(ns, RWKV_H // 2, HEAD_DIM, LANES))


def _unpack_state(s):
    ns = s.shape[0]
    return (s.reshape(ns, RWKV_H // 2, HEAD_DIM, 2, HEAD_DIM).transpose(0, 1, 3, 2, 4)
            .reshape(ns, RWKV_H, HEAD_DIM, HEAD_DIM))


def _rwkv_readout(o, bonus, g, lnx_w, lnx_b):
    avg = _seg_ones(RWKV_W, HEAD_DIM, 1.0 / HEAD_DIM)
    mean = _mm_lhs2(o, avg)
    c = o - mean
    var = _mm_lhs2(c * c, avg)
    return ((c * lax.rsqrt(var + LNX_EPS)) * lnx_w + lnx_b + bonus) * g


def _proj_out_kernel(x_ref, o_ref, bonus_ref, g_ref, oa_ref, ln_ref, mod_ref, w_ref, y_ref):
    d = D_MODEL
    o_rwkv = _rwkv_readout(o_ref[...], bonus_ref[...], g_ref[...], ln_ref[0:1, :], ln_ref[1:2, :])
    mix = jnp.concatenate([o_rwkv, oa_ref[...]], axis=-1).astype(BF16)
    y = jnp.dot(mix, w_ref[...], preferred_element_type=F32)
    y_ref[...] = x_ref[...] + mod_ref[:, 2 * d:3 * d] * y


def proj_out(x, o_scan, bonus, g, o_attn, lnx, mod, w_out_bf16, tm):
    gg, r, d = x.shape
    tok = lambda width: pl.BlockSpec((None, tm, width), lambda g, i: (g, i, 0))
    full = lambda a: pl.BlockSpec(a.shape, lambda g, i: (0,) * a.ndim)
    return pl.pallas_call(
        _proj_out_kernel,
        grid=(gg, r // tm),
        in_specs=[tok(d), tok(RWKV_W), tok(RWKV_W), tok(RWKV_W), tok(ATTN_W), full(lnx),
                  _mod_spec(mod, tm), full(w_out_bf16)],
        out_specs=tok(d),
        out_shape=jax.ShapeDtypeStruct(x.shape, F32),
        compiler_params=_cparams("parallel", "parallel"),
        name="proj_out",
    )(x, o_scan, bonus, g, o_attn, lnx, mod, w_out_bf16)


def rope_tables(pos, width, group, rot_dim, rot_width):
    half = rot_dim // 2
    inv_freq = ROPE_THETA ** (-jnp.arange(half, dtype=F32) / half)
    ang = pos.astype(F32)[:, None] * inv_freq[None, :]
    lane = jnp.arange(width)
    j = lane % group
    rot = (lane < rot_width) & (j < rot_dim)
    cos = jnp.where(rot[None, :], jnp.cos(ang)[:, j % half], 1.0)
    sin = jnp.sin(ang)[:, j % half]
    sin_lo = jnp.where((rot & (j < half))[None, :], -sin, 0.0)
    sin_hi = jnp.where((rot & (j >= half))[None, :], sin, 0.0)
    return jnp.stack([cos, sin_lo, sin_hi]).astype(F32)


def _rope(x, tab_ref, half):
    width = x.shape[-1]
    return (x * tab_ref[0] + pltpu.roll(x, width - half, axis=1) * tab_ref[1]
            + pltpu.roll(x, half, axis=1) * tab_ref[2])


def _attn_prep_kernel(pa_ref, pi_ref, qkw_ref, ta_ref, ti_ref, q_ref, k_ref, qi_ref, ki_ref):
    avg = _seg_ones(ATTN_W, HEAD_DIM, 1.0 / HEAD_DIM)

    def head_norm(x, w):
        return x * lax.rsqrt(_mm_lhs2(x * x, avg) + NORM_EPS) * w

    q = head_norm(pa_ref[:, 0:ATTN_W], qkw_ref[0:1, :])
    k = head_norm(pa_ref[:, ATTN_W:2 * ATTN_W], qkw_ref[1:2, :])
    q_ref[...] = _rope(q, ta_ref, ROT_DIM // 2)
    k_ref[...] = _rope(k, ta_ref, ROT_DIM // 2)
    pi = _rope(pi_ref[...], ti_ref, IDX_ROT_DIM // 2)
    qi_ref[...] = pi[:, 0:IDX_QW]
    ki_ref[...] = pi[:, IDX_QW:]


def attn_prep(pa, pi, qk_norm_w, tab_attn, tab_idx, tm):
    g, r, _ = pa.shape
    tok = lambda width: pl.BlockSpec((None, tm, width), lambda g, i: (g, i, 0))

    def tab_spec(t):
        if t.shape[1] == 1:
            return pl.BlockSpec((3, 1, t.shape[2]), lambda g, i: (0, 0, 0))
        return pl.BlockSpec((3, tm, t.shape[2]), lambda g, i: (0, i, 0))

    widths = (ATTN_W, ATTN_W, IDX_QW, IDX_PAD_W - IDX_QW)
    return pl.pallas_call(
        _attn_prep_kernel,
        grid=(g, r // tm),
        in_specs=[tok(ATTN_IN_W), tok(IDX_PAD_W), pl.BlockSpec(qk_norm_w.shape, lambda g, i: (0, 0)),
                  tab_spec(tab_attn), tab_spec(tab_idx)],
        out_specs=[tok(w) for w in widths],
        out_shape=[jax.ShapeDtypeStruct((g, r, w), F32) for w in widths],
        compiler_params=_cparams("parallel", "parallel"),
        name="attn_prep",
    )(pa, pi, qk_norm_w, tab_attn, tab_idx)


def _sortable_key(score):
    bits = pltpu.bitcast(score + 0.0, I32)
    return bits ^ ((bits >> 31) & 0x7FFFFFFF)


def _kth_largest_key(count_ge, k_sel, shape):
    t = jnp.where(count_ge(jnp.zeros(shape, I32)) >= k_sel, 0, INT_MIN).astype(I32)

    def body(i, t):
        cand = t | jnp.left_shift(jnp.int32(1), 30 - i)
        return jnp.where(count_ge(cand) >= k_sel, cand, t)

    return lax.fori_loop(0, 31, body, t)


def _tie_index_limit(count_eq_below, need, n_bits, shape):
    def body(i, j):
        cand = j | jnp.left_shift(jnp.int32(1), n_bits - 1 - i)
        return jnp.where(count_eq_below(cand) < need, cand, j)

    return lax.fori_loop(0, n_bits, body, jnp.zeros(shape, I32))


def _dsa_prompt_kernel(tq, tk, k_sel, idx_bits, kidx_ref, k_ref, vt_ref, qit_ref, wt_ref, qt_ref, o_ref,
                       ikey_ref, thr_ref, m_ref, l_ref, acc_ref, s_ref, p_ref, alpha_ref):
    qb = pl.program_id(1)
    n_chunks = ((qb + 1) * tq + tk - 1) // tk
    qpos = qb * tq + lax.broadcasted_iota(I32, (tk, tq), 1)
    row = lax.broadcasted_iota(I32, (tk, tq), 0)
    w = wt_ref[...] * (IDX_QW ** -0.5)

    def score_chunk(c, carry):
        rows = pl.ds(pl.multiple_of(c * tk, tk), tk)
        dots = jnp.dot(kidx_ref[rows, :], qit_ref[...], preferred_element_type=F32)
        sc = jnp.zeros((tk, tq), F32)
        for h in range(IDX_H):
            sc = sc + jnp.maximum(dots[:, h * tq:(h + 1) * tq], 0.0) * w[h:h + 1, :]
        sc = jnp.where(c * tk + row <= qpos, sc, -jnp.inf)
        ikey_ref[rows, :] = _sortable_key(sc)
        return carry

    lax.fori_loop(0, n_chunks, score_chunk, 0)

    part = 32

    def fold(x, op):
        return op(x.reshape(tk // part, part, tq), axis=0)

    def count(pred):
        def body(c, acc):
            rows = pl.ds(pl.multiple_of(c * tk, tk), tk)
            hit = pred(ikey_ref[rows, :], c * tk + row)
            return acc + fold(jnp.where(hit, 1.0, 0.0), jnp.sum)
        acc = lax.fori_loop(0, n_chunks, body, jnp.zeros((part, tq), F32))
        return jnp.sum(acc, axis=0, keepdims=True)

    thr = _kth_largest_key(lambda cand: count(lambda key, pos: key >= cand), k_sel, (1, tq))
    need = k_sel - count(lambda key, pos: key > thr)
    n_eq = count(lambda key, pos: key == thr)
    thr_ref[0:1, :] = thr
    thr_ref[1:2, :] = jnp.full((1, tq), 2 ** 30, I32)

    @pl.when(jnp.max(n_eq - need) > 0.0)
    def _():
        thr_ref[1:2, :] = _tie_index_limit(
            lambda cand: count(lambda key, pos: (key == thr) & (pos < cand)), need, idx_bits, (1, tq))

    m_ref[...] = jnp.full(m_ref.shape, -jnp.inf, F32)
    l_ref[...] = jnp.zeros(l_ref.shape, F32)
    acc_ref[...] = jnp.zeros(acc_ref.shape, F32)
    tie_j = thr_ref[1:2, :]

    def attend_chunk(c, carry):
        rows = pl.ds(pl.multiple_of(c * tk, tk), tk)
        key = ikey_ref[rows, :]
        pos = c * tk + row
        sel = ((key > thr) | ((key == thr) & (pos <= tie_j))) & (pos <= qpos)
        bias = jnp.where(sel, 0.0, -jnp.inf)
        for j in range(ATTN_H // 2):
            s_ref[:, 2 * j * tq:2 * (j + 1) * tq] = jnp.dot(
                k_ref[rows, j * LANES:(j + 1) * LANES], qt_ref[j], preferred_element_type=F32)
        for h in range(ATTN_H):
            cols = slice(h * tq, (h + 1) * tq)
            s = s_ref[:, cols] + bias
            m_old = m_ref[h]
            m_new = jnp.maximum(m_old, jnp.max(fold(s, jnp.max), axis=0, keepdims=True))
            m_safe = jnp.where(m_new == -jnp.inf, 0.0, m_new)
            alpha = jnp.exp(m_old - m_safe)
            p = jnp.exp(s - m_safe[0:1, :])
            p_ref[:, cols] = p.astype(BF16)
            l_ref[h] = alpha * l_ref[h] + jnp.sum(fold(p, jnp.sum), axis=0, keepdims=True)
            alpha_ref[h] = alpha
            m_ref[h] = m_new
        for h in range(ATTN_H):
            hs = slice(h * HEAD_DIM, (h + 1) * HEAD_DIM)
            acc_ref[hs, :] = alpha_ref[h][0:1, :] * acc_ref[hs, :] + jnp.dot(
                vt_ref[c, hs, :], p_ref[:, h * tq:(h + 1) * tq], preferred_element_type=F32)
        return carry

    lax.fori_loop(0, n_chunks, attend_chunk, 0)
    for h in range(ATTN_H):
        hs = slice(h * HEAD_DIM, (h + 1) * HEAD_DIM)
        acc_ref[hs, :] = acc_ref[hs, :] / l_ref[h][0:1, :]
    o_ref[...] = acc_ref[...].T


def dsa_prompt(q, k, v, qidx, kidx, widx, tq, tk):
    b, t, _ = q.shape
    nqb, nck = t // tq, t // tk
    k_sel = min(TOPK_MAX, t // 4)
    eye2 = jnp.eye(2, dtype=F32) * (HEAD_DIM ** -0.5)
    q_t = q.reshape(b, nqb, tq, ATTN_H // 2, 2, HEAD_DIM).transpose(0, 1, 3, 4, 5, 2)
    q_t = (q_t[:, :, :, :, None] * eye2[None, None, None, :, :, None, None])
    q_t = q_t.transpose(0, 1, 2, 4, 5, 3, 6).reshape(b, nqb, ATTN_H // 2, LANES, 2 * tq).astype(BF16)
    v_t = v.reshape(b, nck, tk, ATTN_W).transpose(0, 1, 3, 2).astype(BF16)
    qi_t = qidx.reshape(b, nqb, tq, IDX_H, IDX_D).transpose(0, 1, 4, 3, 2).reshape(
        b, nqb, IDX_D, IDX_H * tq).astype(BF16)
    w_t = widx.reshape(b, nqb, tq, IDX_H).transpose(0, 1, 3, 2)
    once = dict(pipeline_mode=pl.Buffered(1))
    return pl.pallas_call(
        functools.partial(_dsa_prompt_kernel, tq, tk, k_sel, (t - 1).bit_length()),
        grid=(b, nqb),
        in_specs=[
            pl.BlockSpec((None, t, IDX_D), lambda b, i: (b, 0, 0), **once),
            pl.BlockSpec((None, t, ATTN_W), lambda b, i: (b, 0, 0), **once),
            pl.BlockSpec((None, nck, ATTN_W, tk), lambda b, i: (b, 0, 0, 0), **once),
            pl.BlockSpec((None, None, IDX_D, IDX_H * tq), lambda b, i: (b, i, 0, 0)),
            pl.BlockSpec((None, None, IDX_H, tq), lambda b, i: (b, i, 0, 0)),
            pl.BlockSpec((None, None, ATTN_H // 2, LANES, 2 * tq), lambda b, i: (b, i, 0, 0, 0)),
        ],
        out_specs=pl.BlockSpec((None, tq, ATTN_W), lambda b, i: (b, i, 0)),
        out_shape=jax.ShapeDtypeStruct((b, t, ATTN_W), F32),
        scratch_shapes=[
            pltpu.VMEM((t, tq), I32),
            pltpu.VMEM((8, tq), I32),
            pltpu.VMEM((ATTN_H, 8, tq), F32),
            pltpu.VMEM((ATTN_H, 8, tq), F32),
            pltpu.VMEM((ATTN_W, tq), F32),
            pltpu.VMEM((tk, ATTN_H * tq), F32),
            pltpu.VMEM((tk, ATTN_H * tq), BF16),
            pltpu.VMEM((ATTN_H, 8, tq), F32),
        ],
        compiler_params=_cparams("parallel", "arbitrary"),
        name="dsa_prompt",
    )(kidx.astype(BF16), k.astype(BF16), v_t, qi_t, w_t, q_t)


_NT = (((1,), (1,)), ((), ()))


def _page_specs(n_pages, layer, block):
    def spec(j):
        return pl.BlockSpec((None, None) + block,
                            lambda b, pt: (layer, pt[b * n_pages + j]) + (0,) * len(block))
    return [spec(j) for j in range(n_pages)]


def _sample_scores_kernel(n_pages, pt_ref, *refs):
    pages, (qi_ref, w_ref, o_ref) = refs[:n_pages], refs[n_pages:]
    qi = qi_ref[...].astype(BF16)
    w = w_ref[...] * (IDX_QW ** -0.5)
    for j in range(n_pages):
        dots = jnp.dot(qi, pages[j][...].astype(BF16), preferred_element_type=F32)
        o_ref[:, j * PAGE_SIZE:(j + 1) * PAGE_SIZE] = jnp.sum(jnp.maximum(dots, 0.0) * w, axis=0, keepdims=True)


def sample_scores(layer, page_table_flat, cache_kidx_t, qidx, widx, n_pages):
    nb = qidx.shape[0]
    return pl.pallas_call(
        functools.partial(_sample_scores_kernel, n_pages),
        grid_spec=pltpu.PrefetchScalarGridSpec(
            num_scalar_prefetch=1,
            grid=(nb,),
            in_specs=_page_specs(n_pages, layer, (IDX_D, PAGE_SIZE)) + [
                pl.BlockSpec((None, IDX_H, IDX_D), lambda b, pt: (b, 0, 0)),
                pl.BlockSpec((None, IDX_H, 1), lambda b, pt: (b, 0, 0)),
            ],
            out_specs=pl.BlockSpec((None, 1, n_pages * PAGE_SIZE), lambda b, pt: (b, 0, 0)),
        ),
        out_shape=jax.ShapeDtypeStruct((nb, 1, n_pages * PAGE_SIZE), F32),
        compiler_params=_cparams("parallel"),
        name="sample_scores",
    )(page_table_flat, *([cache_kidx_t] * n_pages), qidx, widx)


def _sample_select_kernel(k_sel, idx_bits, sc_ref, qi_ref, kt_ref, w_ref, bias_ref, bias_self_ref):
    nb, nk = sc_ref.shape
    r = lax.broadcasted_iota(I32, (IDX_QW, LANES), 0) // IDX_D
    c = lax.broadcasted_iota(I32, (IDX_QW, LANES), 1)
    seg = jnp.where(r == c, 1.0, 0.0).astype(BF16)
    prod = qi_ref[...].astype(BF16).astype(F32) * kt_ref[...].astype(BF16).astype(F32)
    dots = _mm_lhs2(prod, seg)
    self_sc = jnp.sum(jnp.maximum(dots, 0.0) * (w_ref[...] * (IDX_QW ** -0.5)), axis=1, keepdims=True)
    key_self = _sortable_key(self_sc)
    keys = _sortable_key(sc_ref[...])
    pos = lax.broadcasted_iota(I32, (nb, nk), 1)

    def count(pred):
        past = jnp.sum(jnp.where(pred(keys, pos), 1.0, 0.0), axis=1, keepdims=True)
        return past + jnp.where(pred(key_self, nk), 1.0, 0.0)

    thr = _kth_largest_key(lambda cand: count(lambda key, p: key >= cand), k_sel, (nb, 1))
    need = k_sel - count(lambda key, p: key > thr)
    tie_j = _tie_index_limit(
        lambda cand: count(lambda key, p: (key == thr) & (p < cand)), need, idx_bits, (nb, 1))
    chosen = lambda key, p: (key > thr) | ((key == thr) & (p <= tie_j))
    bias_ref[...] = jnp.where(chosen(keys, pos), 0.0, -jnp.inf)
    bias_self_ref[...] = jnp.broadcast_to(jnp.where(chosen(key_self, nk), 0.0, -jnp.inf), (nb, LANES))


def sample_select(scores, qidx, kidx_tiled, w_pad):
    nb, nk = scores.shape
    k_sel = min(TOPK_MAX, (nk + 1) // 4)
    return pl.pallas_call(
        functools.partial(_sample_select_kernel, k_sel, nk.bit_length()),
        out_shape=[jax.ShapeDtypeStruct((nb, nk), F32), jax.ShapeDtypeStruct((nb, LANES), F32)],
        compiler_params=pltpu.CompilerParams(vmem_limit_bytes=VMEM_LIMIT),
        name="sample_select",
    )(scores, qidx, kidx_tiled, w_pad)


def _sample_attend_kernel(n_pages, pt_ref, *refs):
    k_pages, v_pages = refs[:n_pages], refs[n_pages:2 * n_pages]
    bias_ref, q_ref, kn_ref, vn_ref, bself_ref, o_ref, s_ref = refs[2 * n_pages:]
    own = (lax.broadcasted_iota(I32, (ATTN_H, ATTN_W), 1) // HEAD_DIM
           == lax.broadcasted_iota(I32, (ATTN_H, ATTN_W), 0))
    q_bd = jnp.where(own, q_ref[...], 0.0) * (HEAD_DIM ** -0.5)
    q_bf = q_bd.astype(BF16)
    for j in range(n_pages):
        cols = slice(j * PAGE_SIZE, (j + 1) * PAGE_SIZE)
        k_t = k_pages[j][...].reshape(ATTN_W, PAGE_SIZE).astype(BF16)
        s_ref[:, cols] = jnp.dot(q_bf, k_t, preferred_element_type=F32) + bias_ref[:, cols]
    s = s_ref[...]
    s_self = jnp.sum(q_bd * kn_ref[...], axis=1, keepdims=True) + bself_ref[:, 0:1]
    m = jnp.maximum(jnp.max(s, axis=1, keepdims=True), s_self)
    p = jnp.exp(s - m)
    p_self = jnp.exp(s_self - m)
    denom = jnp.sum(p, axis=1, keepdims=True) + p_self
    acc = p_self * vn_ref[...]
    for j in range(n_pages):
        v_t = v_pages[j][...].reshape(ATTN_W, PAGE_SIZE).astype(BF16)
        acc = acc + lax.dot_general(p[:, j * PAGE_SIZE:(j + 1) * PAGE_SIZE].astype(BF16), v_t, _NT,
                                    preferred_element_type=F32)
    o_ref[...] = jnp.sum(jnp.where(own, acc / denom, 0.0), axis=0, keepdims=True)


def sample_attend(layer, page_table_flat, cache_k_t, cache_v_t, bias, bias_self, q, k_new, v_new, n_pages):
    nb = q.shape[0]
    past = n_pages * PAGE_SIZE
    row = pl.BlockSpec((None, 1, ATTN_W), lambda b, pt: (b, 0, 0))
    pages = _page_specs(n_pages, layer, (ATTN_H, HEAD_DIM, PAGE_SIZE))
    return pl.pallas_call(
        functools.partial(_sample_attend_kernel, n_pages),
        grid_spec=pltpu.PrefetchScalarGridSpec(
            num_scalar_prefetch=1,
            grid=(nb,),
            in_specs=pages + pages + [
                pl.BlockSpec((None, 1, past), lambda b, pt: (b, 0, 0)),
                row, row, row,
                pl.BlockSpec((None, 1, LANES), lambda b, pt: (b, 0, 0))],
            out_specs=row,
            scratch_shapes=[pltpu.VMEM((ATTN_H, past), F32)],
        ),
        out_shape=jax.ShapeDtypeStruct((nb, 1, ATTN_W), F32),
        compiler_params=_cparams("parallel"),
        name="sample_attend",
    )(page_table_flat, *([cache_k_t] * n_pages), *([cache_v_t] * n_pages), bias, q, k_new, v_new, bias_self)


def _ffn_kernel(routed, x_ref, nw_ref, mod_ref, router_ref, wg_ref, wu_ref, wd_ref, y_ref,
                h_ref, acc_ref, comb_ref):
    d = D_MODEL
    j = pl.program_id(2)

    @pl.when(j == 0)
    def _():
        h = _modulated_norm(x_ref[...], nw_ref[...], mod_ref[:, 3 * d:4 * d], mod_ref[:, 4 * d:5 * d])
        h_ref[...] = h.astype(BF16)
        acc_ref[...] = jnp.zeros(acc_ref.shape, F32)
        if routed:
            lane = lax.broadcasted_iota(I32, comb_ref.shape, 1).astype(F32)
            logits = jnp.where(lane < N_EXPERTS, _mm3(h, router_ref[...]), -jnp.inf)
            m1 = jnp.max(logits, axis=1, keepdims=True)
            i1 = jnp.min(jnp.where(logits == m1, lane, float(LANES)), axis=1, keepdims=True)
            rest = jnp.where(lane == i1, -jnp.inf, logits)
            m2 = jnp.max(rest, axis=1, keepdims=True)
            i2 = jnp.min(jnp.where(rest == m2, lane, float(LANES)), axis=1, keepdims=True)
            e2 = jnp.exp(m2 - m1)
            comb_ref[...] = (jnp.where(lane == i1, 1.0 / (1.0 + e2), 0.0)
                             + jnp.where(lane == i2, e2 / (1.0 + e2), 0.0))

    h = h_ref[...]
    t = _silu(jnp.dot(h, wg_ref[...], preferred_element_type=F32)) * jnp.dot(
        h, wu_ref[...], preferred_element_type=F32)
    y = jnp.dot(t.astype(BF16), wd_ref[...], preferred_element_type=F32)
    if routed:
        lane = lax.broadcasted_iota(I32, comb_ref.shape, 1)
        y = y * jnp.sum(jnp.where(lane == j, comb_ref[...], 0.0), axis=1, keepdims=True)
    acc_ref[...] += y

    @pl.when(j == pl.num_programs(2) - 1)
    def _():
        y_ref[...] = x_ref[...] + mod_ref[:, 5 * d:6 * d] * acc_ref[...]


def ffn(x, norm_w, mod, router_pad, wg, wu, wd, tm, routed):
    g, r, d = x.shape
    nj, _, f = wg.shape
    tok = pl.BlockSpec((None, tm, d), lambda g, i, j: (g, i, 0))
    if mod.shape[1] == 1:
        mod_spec = pl.BlockSpec((None, 1, mod.shape[2]), lambda g, i, j: (g, 0, 0))
    else:
        mod_spec = pl.BlockSpec((None, tm, mod.shape[2]), lambda g, i, j: (g, i, 0))
    return pl.pallas_call(
        functools.partial(_ffn_kernel, routed),
        grid=(g, r // tm, nj),
        in_specs=[tok, pl.BlockSpec((1, d), lambda g, i, j: (0, 0)), mod_spec,
                  pl.BlockSpec(router_pad.shape, lambda g, i, j: (0, 0)),
                  pl.BlockSpec((None, d, f), lambda g, i, j: (j, 0, 0)),
                  pl.BlockSpec((None, d, f), lambda g, i, j: (j, 0, 0)),
                  pl.BlockSpec((None, f, d), lambda g, i, j: (j, 0, 0))],
        out_specs=tok,
        out_shape=jax.ShapeDtypeStruct(x.shape, F32),
        scratch_shapes=[pltpu.VMEM((tm, d), BF16), pltpu.VMEM((tm, d), F32), pltpu.VMEM((tm, LANES), F32)],
        compiler_params=_cparams("parallel", "parallel", "arbitrary"),
        name="ffn_routed" if routed else "ffn_dense",
    )(x, norm_w.reshape(1, d), mod, router_pad, wg, wu, wd)


def _layer_params(l, p):
    z = jnp.zeros((LORA_W, RWKV_W), F32)
    w_in = jnp.pad(p["w_in"][l], ((0, 0), (0, SHIFT_W + ATTN_IN_W + IDX_PAD_W - IN_W))).astype(BF16)
    j = l // 2
    if l % 2 == 0:
        halves = lambda w: jnp.stack(jnp.split(w, w.shape[1] // D_FF_TILE, axis=1)).astype(BF16)
        wg, wu = halves(p["ffn_w_gate"][j]), halves(p["ffn_w_up"][j])
        wd = jnp.stack(jnp.split(p["ffn_w_down"][j], wg.shape[0], axis=0)).astype(BF16)
        router = jnp.zeros((8, LANES), F32)
    else:
        wg, wu, wd = (p[n][j].astype(BF16) for n in ("exp_w_gate", "exp_w_up", "exp_w_down"))
        router = jnp.pad(p["router_w"][j], ((0, 0), (0, LANES - N_EXPERTS)))
    return dict(
        norm1_w=p["norm1_w"][l], norm2_w=p["norm2_w"][l], w_in=w_in, mu=p["mu_shift"][l][None, :],
        vecs=jnp.stack([p["rwkv_w0"][l], p["rwkv_a0"][l], p["rwkv_k_k"][l], p["rwkv_k_a"][l],
                        p["rwkv_r_k"][l].reshape(RWKV_W)] + [jnp.zeros((RWKV_W,), F32)] * 3),
        w2=z.at[0:W_LORA].set(p["rwkv_w2"][l]),
        a2=z.at[W_LORA:W_LORA + A_LORA].set(p["rwkv_a2"][l]),
        g2=z.at[W_LORA + A_LORA:].set(p["rwkv_g2"][l]),
        lnx=jnp.stack([p["lnx_w"][l], p["lnx_b"][l]]),
        qkw=jnp.stack([jnp.tile(p["q_norm_w"][l], ATTN_H), jnp.tile(p["k_norm_w"][l], ATTN_H)]),
        w_out=p["w_out"][l].astype(BF16), router=router, wg=wg, wu=wu, wd=wd, routed=(l % 2 == 1))


def _trunk_layer(x, mod, lp, shift_prev, s0, tabs, seq_mode, tm, attend):
    g, r, _ = x.shape
    pr, pa, pi = proj_in(x, lp["norm1_w"], mod, lp["w_in"], tm)
    rr, dd, kr, vr, kk, bb_, gate, bonus = rwkv_prep(pr, shift_prev, lp["mu"], lp["vecs"], lp["w2"], lp["a2"],
                                                     lp["g2"], tm, seq_mode)
    if seq_mode:
        seqs = lambda a: a
        v_t = vr.reshape(g, r, RWKV_H, HEAD_DIM).transpose(0, 2, 3, 1)
        o_t, s_fin = rwkv_scan(rr, dd, kr, kk, bb_, v_t, _pack_state(s0), g, LANES)
        o_scan = o_t.transpose(0, 3, 1, 2).reshape(g, r, RWKV_W)
    else:
        seqs = lambda a: a.reshape(r, 1, RWKV_W)
        o_t, s_fin = rwkv_scan(seqs(rr), seqs(dd), seqs(kr), seqs(kk), seqs(bb_),
                               vr.reshape(r, RWKV_H, HEAD_DIM, 1), _pack_state(s0), 8, 1)
        o_scan = o_t.reshape(g, r, RWKV_W)
    q, k, qi, ki = attn_prep(pa, pi, lp["qkw"], tabs[0], tabs[1], tm)
    v = pa[..., 2 * ATTN_W:]
    kidx = ki[..., :IDX_D]
    widx = ki[..., IDX_D:IDX_D + IDX_H]
    o_attn = attend(q, k, v, qi, kidx, widx)
    x = proj_out(x, o_scan, bonus, gate, o_attn, lp["lnx"], mod, lp["w_out"], tm)
    x = ffn(x, lp["norm2_w"], mod, lp["router"], lp["wg"], lp["wu"], lp["wd"], min(512, r), lp["routed"])
    return x, (k, v, kidx, _unpack_state(s_fin), pr[:, -1] if seq_mode else pr[0])


def kernel(x_prompt, x_sample, c_prompt, c_sample, cache_k, cache_v, cache_kidx, state_wkv, state_shift, page_table, norm1_w, norm2_w, w_ada, b_ada, w_in, mu_shift, rwkv_w0, rwkv_w2, rwkv_a0, rwkv_a2, rwkv_g2, rwkv_k_k, rwkv_k_a, rwkv_r_k, lnx_w, lnx_b, q_norm_w, k_norm_w, w_out, ffn_w_gate, ffn_w_up, ffn_w_down, router_w, exp_w_gate, exp_w_up, exp_w_down):
    params = dict(norm1_w=norm1_w, norm2_w=norm2_w, w_in=w_in, mu_shift=mu_shift, rwkv_w0=rwkv_w0,
                  rwkv_w2=rwkv_w2, rwkv_a0=rwkv_a0, rwkv_a2=rwkv_a2, rwkv_g2=rwkv_g2, rwkv_k_k=rwkv_k_k,
                  rwkv_k_a=rwkv_k_a, rwkv_r_k=rwkv_r_k, lnx_w=lnx_w, lnx_b=lnx_b, q_norm_w=q_norm_w,
                  k_norm_w=k_norm_w, w_out=w_out, ffn_w_gate=ffn_w_gate, ffn_w_up=ffn_w_up,
                  ffn_w_down=ffn_w_down, router_w=router_w, exp_w_gate=exp_w_gate, exp_w_up=exp_w_up,
                  exp_w_down=exp_w_down)
    depth = w_in.shape[0]
    bp, t, d = x_prompt.shape
    nb = x_sample.shape[0]
    n_pages = page_table.shape[1]
    past = n_pages * PAGE_SIZE
    pt_flat = page_table.reshape(-1).astype(I32)
    cache_k_t = jnp.transpose(cache_k, (0, 1, 3, 4, 2))
    cache_v_t = jnp.transpose(cache_v, (0, 1, 3, 4, 2))
    cache_kidx_t = jnp.transpose(cache_kidx, (0, 1, 3, 2))

    n_mod = bp + nb
    c_all = jnp.pad(jnp.concatenate([c_prompt, c_sample]), ((0, -n_mod % 8), (0, 0)))
    mod = ada_modulation(c_all, w_ada, b_ada)
    mod_p = mod[:, :bp, None, :]
    mod_s = mod[:, None, bp:n_mod, :]

    def tables(pos):
        return (rope_tables(pos, ATTN_W, HEAD_DIM, ROT_DIM, ATTN_W),
                rope_tables(pos, IDX_PAD_W, IDX_D, IDX_ROT_DIM, IDX_QW + IDX_D))

    tabs_p = tables(jnp.arange(t, dtype=I32))
    tabs_s = tables(jnp.full((1,), past, I32))
    tm_p = min(256, t)
    tq, tk = min(128, t), min(512, t)

    xp, xs = x_prompt, x_sample.reshape(1, nb, d)
    st_p, st_s = [], []
    for l in range(depth):
        lp = _layer_params(l, params)
        attend_p = lambda q, k, v, qi, ki, wi: dsa_prompt(q, k, v, qi, ki, wi, tq, tk)

        def attend_s(q, k, v, qi, ki, wi, l=l):
            rows = lambda a: a.reshape(nb, 1, ATTN_W)
            qi_h = qi.reshape(nb, IDX_H, IDX_D)
            scores = sample_scores(l, pt_flat, cache_kidx_t, qi_h, wi.reshape(nb, IDX_H, 1), n_pages)
            w_pad = jnp.pad(wi.reshape(nb, IDX_H), ((0, 0), (0, LANES - IDX_H)))
            bias, bias_self = sample_select(scores.reshape(nb, past), qi.reshape(nb, IDX_QW),
                                            jnp.tile(ki.reshape(nb, IDX_D), (1, IDX_H)), w_pad)
            o = sample_attend(l, pt_flat, cache_k_t, cache_v_t, bias.reshape(nb, 1, past),
                              bias_self.reshape(nb, 1, LANES), rows(q), rows(k), rows(v), n_pages)
            return o.reshape(1, nb, ATTN_W)

        xp, sp = _trunk_layer(xp, mod_p[l], lp, jnp.zeros((bp, 1, SHIFT_W), F32),
                              jnp.zeros((bp, RWKV_H, HEAD_DIM, HEAD_DIM), F32), tabs_p, True, tm_p, attend_p)
        xs, ss = _trunk_layer(xs, mod_s[l], lp, state_shift[l][None], state_wkv[l], tabs_s, False, nb, attend_s)
        st_p.append(sp)
        st_s.append(ss)

    def stack(states, i, shape):
        return jnp.stack([s[i] for s in states]).reshape(shape)

    return (xp, xs.reshape(nb, 1, d),
            stack(st_p, 0, (depth, bp, t, ATTN_H, HEAD_DIM)), stack(st_p, 1, (depth, bp, t, ATTN_H, HEAD_DIM)),
            stack(st_p, 2, (depth, bp, t, IDX_D)), stack(st_p, 3, (depth, bp, RWKV_H, HEAD_DIM, HEAD_DIM)),
            stack(st_p, 4, (depth, bp, SHIFT_W)),
            stack(st_s, 0, (depth, nb, 1, ATTN_H, HEAD_DIM)), stack(st_s, 1, (depth, nb, 1, ATTN_H, HEAD_DIM)),
            stack(st_s, 2, (depth, nb, 1, IDX_D)), stack(st_s, 3, (depth, nb, RWKV_H, HEAD_DIM, HEAD_DIM)),
            stack(st_s, 4, (depth, nb, SHIFT_W)))
```

```python
import functools

import jax
import jax.numpy as jnp
from jax import lax
from jax.experimental import pallas as pl
from jax.experimental.pallas import tpu as pltpu

F32 = jnp.float32
BF16 = jnp.bfloat16
I32 = jnp.int32

D_MODEL = 1024
HEAD_DIM = 64
RWKV_W = 512
RWKV_H = 8
ATTN_W = 512
ATTN_H = 8
W_LORA = 64
A_LORA = 64
G_LORA = 128
LORA_W = W_LORA + A_LORA + G_LORA
IDX_H = 8
IDX_D = 32
IDX_QW = IDX_H * IDX_D
TOPK_MAX = 256
ROT_DIM = 16
IDX_ROT_DIM = 8
ROPE_THETA = 500000.0
PAGE_SIZE = 128
D_FF_TILE = 1408
N_EXPERTS = 8
NORM_EPS = 1e-6
LNX_EPS = 64e-5
SHIFT_W = 3 * RWKV_W + LORA_W
ATTN_IN_W = ATTN_W + 2 * ATTN_W
IDX_IN_W = IDX_QW + IDX_D + IDX_H
IDX_PAD_W = 384
IN_W = SHIFT_W + ATTN_IN_W + IDX_IN_W

LANES = 128
VMEM_LIMIT = 56 * 1024 * 1024
INT_MIN = -(2 ** 31)
LOG2_E = 1.4426950408889634


def _cparams(*sem):
    return pltpu.CompilerParams(dimension_semantics=sem, vmem_limit_bytes=VMEM_LIMIT)


def _mm(a, b):
    return jnp.dot(a.astype(BF16), b.astype(BF16), preferred_element_type=F32)


def _split2(a):
    hi = a.astype(BF16)
    lo = (a - hi.astype(F32)).astype(BF16)
    return hi, lo


def _mm_lhs2(a, b_bf16):
    hi, lo = _split2(a)
    return (jnp.dot(hi, b_bf16, preferred_element_type=F32)
            + jnp.dot(lo, b_bf16, preferred_element_type=F32))


def _mm3(a, b):
    a_hi, a_lo = _split2(a)
    b_hi, b_lo = _split2(b)
    return (jnp.dot(a_hi, b_hi, preferred_element_type=F32)
            + jnp.dot(a_lo, b_hi, preferred_element_type=F32)
            + jnp.dot(a_hi, b_lo, preferred_element_type=F32))


def _silu(x):
    return x * jax.nn.sigmoid(x)


def _seg_ones(width, seg, scale=1.0):
    r = lax.broadcasted_iota(I32, (width, width), 0) // seg
    c = lax.broadcasted_iota(I32, (width, width), 1) // seg
    return jnp.where(r == c, scale, 0.0).astype(BF16)


def _ada_kernel(c_ref, w_ref, b_ref, o_ref):
    o_ref[...] = _mm3(_silu(c_ref[...]), w_ref[...]) + b_ref[...]


def ada_modulation(c_all, w_ada, b_ada):
    m, d = c_all.shape
    nl, _, n = w_ada.shape
    tn = 1536
    return pl.pallas_call(
        _ada_kernel,
        grid=(nl, n // tn),
        in_specs=[
            pl.BlockSpec((m, d), lambda l, j: (0, 0)),
            pl.BlockSpec((None, d, tn), lambda l, j: (l, 0, j)),
            pl.BlockSpec((None, 1, tn), lambda l, j: (l, 0, j)),
        ],
        out_specs=pl.BlockSpec((None, m, tn), lambda l, j: (l, 0, j)),
        out_shape=jax.ShapeDtypeStruct((nl, m, n), F32),
        compiler_params=_cparams("parallel", "parallel"),
        name="ada_modulation",
    )(c_all, w_ada, b_ada.reshape(nl, 1, n))


def _modulated_norm(x, norm_w, shift, scale):
    y = x * lax.rsqrt(jnp.mean(x * x, axis=-1, keepdims=True) + NORM_EPS)
    return (y * norm_w) * (1.0 + scale) + shift


def _proj_in_kernel(x_ref, nw_ref, mod_ref, w_ref, pr_ref, pa_ref, pi_ref):
    d = D_MODEL
    h = _modulated_norm(x_ref[...], nw_ref[...], mod_ref[:, 0:d], mod_ref[:, d:2 * d])
    p = jnp.dot(h.astype(BF16), w_ref[...], preferred_element_type=F32)
    pr_ref[...] = p[:, :SHIFT_W]
    pa_ref[...] = p[:, SHIFT_W:SHIFT_W + ATTN_IN_W]
    pi_ref[...] = p[:, SHIFT_W + ATTN_IN_W:]


def _mod_spec(mod, tm):
    if mod.shape[1] == 1:
        return pl.BlockSpec((None, 1, mod.shape[2]), lambda g, i: (g, 0, 0))
    return pl.BlockSpec((None, tm, mod.shape[2]), lambda g, i: (g, i, 0))


def proj_in(x, norm_w, mod, w_in_bf16, tm):
    g, r, d = x.shape
    n = w_in_bf16.shape[1]
    widths = (SHIFT_W, ATTN_IN_W, n - SHIFT_W - ATTN_IN_W)
    return pl.pallas_call(
        _proj_in_kernel,
        grid=(g, r // tm),
        in_specs=[
            pl.BlockSpec((None, tm, d), lambda g, i: (g, i, 0)),
            pl.BlockSpec((1, d), lambda g, i: (0, 0)),
            _mod_spec(mod, tm),
            pl.BlockSpec((d, n), lambda g, i: (0, 0)),
        ],
        out_specs=[pl.BlockSpec((None, tm, w), lambda g, i: (g, i, 0)) for w in widths],
        out_shape=[jax.ShapeDtypeStruct((g, r, w), F32) for w in widths],
        compiler_params=_cparams("parallel", "parallel"),
        name="proj_in",
    )(x, norm_w.reshape(1, d), mod, w_in_bf16)


def _rwkv_prep_kernel(seq_mode, p_ref, halo_ref, sp_ref, mu_ref, vec_ref, w2_ref, a2_ref, g2_ref,
                      dr_ref, d_ref, k_ref, v_ref, kk_ref, b_ref, g_ref, bonus_ref, kr_ref):
    p = p_ref[...]
    if seq_mode:
        first = jnp.where(pl.program_id(1) == 0, sp_ref[...], halo_ref[7:8, :])
        row = lax.broadcasted_iota(I32, p.shape, 0)
        prev = jnp.where(row == 0, first, pltpu.roll(p, 1, axis=0))
    else:
        prev = sp_ref[...]
    xs = p + mu_ref[...] * (prev - p)
    w = RWKV_W
    r = xs[:, 0:w]
    k = xs[:, w:2 * w]
    v = xs[:, 2 * w:3 * w]
    lo = xs[:, 3 * w:]
    lane = lax.broadcasted_iota(I32, lo.shape, 1)
    act = jnp.where(lane < W_LORA, jnp.tanh(lo),
                    jnp.where(lane < W_LORA + A_LORA, lo, jax.nn.sigmoid(lo)))
    w0, a0, k_k, k_a, r_k = (vec_ref[i:i + 1, :] for i in range(5))
    z = -(w0 + _mm3(act, w2_ref[...]))
    softplus = jnp.maximum(z, 0.0) + jnp.log(1.0 + jnp.exp(-jnp.abs(z)))
    decay = jnp.exp(-jnp.exp(-softplus - 0.5))
    a = jax.nn.sigmoid(a0 + _mm3(act, a2_ref[...]))
    g = _mm3(act, g2_ref[...])
    seg = _seg_ones(w, HEAD_DIM)
    kk = k * k_k
    kk = kk * lax.rsqrt(jnp.maximum(_mm_lhs2(kk * kk, seg), 1e-24))
    k2 = k * (1.0 + (a - 1.0) * k_a)
    b = kk * a
    dr_ref[...] = decay * r - _mm_lhs2(b * r, seg) * kk
    d_ref[...] = decay
    k_ref[...] = k2
    v_ref[...] = v
    kk_ref[...] = kk
    b_ref[...] = b
    g_ref[...] = g
    bonus_ref[...] = _mm_lhs2(r * k2 * r_k, seg) * v
    kr_ref[...] = _mm_lhs2(k2 * r, seg)


def rwkv_prep(p, shift_prev, mu, vecs, w2p, a2p, g2p, tm, seq_mode):
    g, r, sw = p.shape
    w = RWKV_W
    tok = lambda width: pl.BlockSpec((None, tm, width), lambda g, i: (g, i, 0))
    full = lambda a: pl.BlockSpec(a.shape, lambda g, i: (0,) * a.ndim)
    if seq_mode:
        halo_spec = pl.BlockSpec((None, 8, sw), lambda g, i: (g, jnp.maximum(i * (tm // 8) - 1, 0), 0))
        sp_spec = pl.BlockSpec((None, 1, sw), lambda g, i: (g, 0, 0))
    else:
        halo_spec = pl.BlockSpec((None, 8, sw), lambda g, i: (g, 0, 0))
        sp_spec = tok(sw)
    return pl.pallas_call(
        functools.partial(_rwkv_prep_kernel, seq_mode),
        grid=(g, r // tm),
        in_specs=[tok(sw), halo_spec, sp_spec, full(mu), full(vecs), full(w2p), full(a2p), full(g2p)],
        out_specs=[tok(w)] * 9,
        out_shape=[jax.ShapeDtypeStruct((g, r, w), F32)] * 9,
        compiler_params=_cparams("parallel", "arbitrary"),
        name="rwkv_prep",
    )(p, p, shift_prev, mu, vecs, w2p, a2p, g2p)


OUT_WINDOW = 64


def _rwkv_scan_kernel(bb_n, tt, dr_ref, d_ref, k_ref, kk_ref, b_ref, kr_ref, vt_ref, s0_ref,
                      ot_ref, st_ref):
    @pl.when(pl.program_id(1) == 0)
    def _():
        st_ref[...] = s0_ref[...]

    ot_ref[...] = jnp.zeros_like(ot_ref)
    lane = lax.broadcasted_iota(I32, (HEAD_DIM, LANES), 1)
    lane_a = lane < HEAD_DIM
    t_lane = lax.broadcasted_iota(I32, (HEAD_DIM, tt), 1)
    pairs = [(bb, j) for bb in range(bb_n) for j in range(RWKV_H // 2)]
    ones_r = lax.broadcasted_iota(I32, (2 * LANES, LANES), 0) % LANES // HEAD_DIM
    ones_c = lax.broadcasted_iota(I32, (2 * LANES, LANES), 1) // HEAD_DIM
    stacked_ones = jnp.where(ones_r == ones_c, 1.0, 0.0).astype(BF16)

    def head_sums(x):
        hi, lo = _split2(x)
        return jnp.dot(jnp.concatenate([hi, lo], axis=1), stacked_ones, preferred_element_type=F32)

    def step(t, rows_of, u):
        hot = t_lane == t
        hot_out = (lane % OUT_WINDOW) == (t % OUT_WINDOW)
        window = t // OUT_WINDOW

        def rows_all(ref):
            return jnp.concatenate(
                [jnp.broadcast_to(rows_of(ref, bb, slice(j * LANES, (j + 1) * LANES))[u:u + 1, :],
                                  (HEAD_DIM, LANES)) for bb, j in pairs], axis=0)

        def value_cols(bb, j):
            v_a = jnp.sum(jnp.where(hot, vt_ref[bb, 2 * j], 0.0), axis=1, keepdims=True)
            v_b = jnp.sum(jnp.where(hot, vt_ref[bb, 2 * j + 1], 0.0), axis=1, keepdims=True)
            return jnp.where(lane_a, v_a, v_b)

        s = jnp.concatenate([st_ref[bb, j] for bb, j in pairs], axis=0)
        sk = head_sums(s * rows_all(kk_ref))
        s_dr = head_sums(s * rows_all(dr_ref))
        vv = jnp.concatenate([value_cols(bb, j) for bb, j in pairs], axis=0)
        s_new = s * rows_all(d_ref) - sk * rows_all(b_ref) + vv * rows_all(k_ref)
        out = s_dr + rows_all(kr_ref) * vv
        for i, (bb, j) in enumerate(pairs):
            rows = slice(i * HEAD_DIM, (i + 1) * HEAD_DIM)
            st_ref[bb, j] = s_new[rows]
            ot_ref[bb, j, window] = jnp.where(hot_out, out[rows], ot_ref[bb, j, window])

    sub = 8
    if tt % sub:
        for t in range(tt):
            step(t, lambda ref, bb, ls: ref[bb, :, ls], t)
    else:
        def group(g, carry):
            base = pl.multiple_of(g * sub, sub)
            for u in range(sub):
                step(base + u, lambda ref, bb, ls: ref[bb, pl.ds(base, sub), ls], u)
            return carry

        lax.fori_loop(0, tt // sub, group, 0)


def rwkv_scan(row_operands, v_t, s0_packed, bb, tt):
    ns, t, w = row_operands[0].shape
    n_win, win_blk = pl.cdiv(t, OUT_WINDOW), pl.cdiv(tt, OUT_WINDOW)
    row_spec = pl.BlockSpec((bb, tt, w), lambda n, i: (n, i, 0))
    vt_spec = pl.BlockSpec((bb, RWKV_H, HEAD_DIM, tt), lambda n, i: (n, 0, 0, i))
    st_spec = pl.BlockSpec((bb, RWKV_H // 2, HEAD_DIM, LANES), lambda n, i: (n, 0, 0, 0))
    ot_spec = pl.BlockSpec((bb, RWKV_H // 2, win_blk, HEAD_DIM, LANES), lambda n, i: (n, 0, i, 0, 0))
    return pl.pallas_call(
        functools.partial(_rwkv_scan_kernel, bb, tt),
        grid=(ns // bb, t // tt),
        in_specs=[row_spec] * len(row_operands) + [vt_spec, st_spec],
        out_specs=[ot_spec, st_spec],
        out_shape=[jax.ShapeDtypeStruct((ns, RWKV_H // 2, n_win, HEAD_DIM, LANES), F32),
                   jax.ShapeDtypeStruct(s0_packed.shape, F32)],
        compiler_params=_cparams("parallel", "arbitrary"),
        name="rwkv_scan",
    )(*row_operands, v_t, s0_packed)


def _readout_rows(o_tiles, t):
    ns, _, n_win = o_tiles.shape[:3]
    o = o_tiles.reshape(ns, RWKV_H // 2, n_win, HEAD_DIM, 2, OUT_WINDOW).transpose(0, 2, 5, 1, 4, 3)
    return o.reshape(ns, n_win * OUT_WINDOW, RWKV_W)[:, :t]


def _pack_state(s):
    ns = s.shape[0]
    return (s.reshape(ns, RWKV_H // 2, 2, HEAD_DIM, HEAD_DIM).transpose(0, 1, 3, 2, 4)
            .reshape(ns, RWKV_H // 2, HEAD_DIM, LANES))


def _unpack_state(s):
    ns = s.shape[0]
    return (s.reshape(ns, RWKV_H // 2, HEAD_DIM, 2, HEAD_DIM).transpose(0, 1, 3, 2, 4)
            .reshape(ns, RWKV_H, HEAD_DIM, HEAD_DIM))


def _rwkv_readout(o, bonus, g, lnx_w, lnx_b):
    avg = _seg_ones(RWKV_W, HEAD_DIM, 1.0 / HEAD_DIM)
    mean = _mm_lhs2(o, avg)
    c = o - mean
    var = _mm_lhs2(c * c, avg)
    return ((c * lax.rsqrt(var + LNX_EPS)) * lnx_w + lnx_b + bonus) * g


def _proj_out_kernel(x_ref, o_ref, bonus_ref, g_ref, oa_ref, ln_ref, mod_ref, w_ref, y_ref):
    d = D_MODEL
    o_rwkv = _rwkv_readout(o_ref[...], bonus_ref[...], g_ref[...], ln_ref[0:1, :], ln_ref[1:2, :])
    mix = jnp.concatenate([o_rwkv, oa_ref[...]], axis=-1).astype(BF16)
    y = jnp.dot(mix, w_ref[...], preferred_element_type=F32)
    y_ref[...] = x_ref[...] + mod_ref[:, 2 * d:3 * d] * y


def proj_out(x, o_scan, bonus, g, o_attn, lnx, mod, w_out_bf16, tm):
    gg, r, d = x.shape
    tok = lambda width: pl.BlockSpec((None, tm, width), lambda g, i: (g, i, 0))
    full = lambda a: pl.BlockSpec(a.shape, lambda g, i: (0,) * a.ndim)
    return pl.pallas_call(
        _proj_out_kernel,
        grid=(gg, r // tm),
        in_specs=[tok(d), tok(RWKV_W), tok(RWKV_W), tok(RWKV_W), tok(ATTN_W), full(lnx),
                  _mod_spec(mod, tm), full(w_out_bf16)],
        out_specs=tok(d),
        out_shape=jax.ShapeDtypeStruct(x.shape, F32),
        compiler_params=_cparams("parallel", "parallel"),
        name="proj_out",
    )(x, o_scan, bonus, g, o_attn, lnx, mod, w_out_bf16)


def rope_tables(pos, width, group, rot_dim, rot_width):
    half = rot_dim // 2
    inv_freq = ROPE_THETA ** (-jnp.arange(half, dtype=F32) / half)
    ang = pos.astype(F32)[:, None] * inv_freq[None, :]
    lane = jnp.arange(width)
    j = lane % group
    rot = (lane < rot_width) & (j < rot_dim)
    cos = jnp.where(rot[None, :], jnp.cos(ang)[:, j % half], 1.0)
    sin = jnp.sin(ang)[:, j % half]
    sin_lo = jnp.where((rot & (j < half))[None, :], -sin, 0.0)
    sin_hi = jnp.where((rot & (j >= half))[None, :], sin, 0.0)
    return jnp.stack([cos, sin_lo, sin_hi]).astype(F32)


def _rope(x, tab_ref, half):
    width = x.shape[-1]
    return (x * tab_ref[0] + pltpu.roll(x, width - half, axis=1) * tab_ref[1]
            + pltpu.roll(x, half, axis=1) * tab_ref[2])


def _attn_prep_kernel(pa_ref, pi_ref, qkw_ref, ta_ref, ti_ref, q_ref, k_ref, qi_ref, ki_ref):
    avg = _seg_ones(ATTN_W, HEAD_DIM, 1.0 / HEAD_DIM)

    def head_norm(x, w):
        return x * lax.rsqrt(_mm_lhs2(x * x, avg) + NORM_EPS) * w

    q = head_norm(pa_ref[:, 0:ATTN_W], qkw_ref[0:1, :])
    k = head_norm(pa_ref[:, ATTN_W:2 * ATTN_W], qkw_ref[1:2, :])
    q_ref[...] = _rope(q, ta_ref, ROT_DIM // 2)
    k_ref[...] = _rope(k, ta_ref, ROT_DIM // 2)
    pi = _rope(pi_ref[...], ti_ref, IDX_ROT_DIM // 2)
    qi_ref[...] = pi[:, 0:IDX_QW]
    ki_ref[...] = pi[:, IDX_QW:]


def attn_prep(pa, pi, qk_norm_w, tab_attn, tab_idx, tm):
    g, r, _ = pa.shape
    tok = lambda width: pl.BlockSpec((None, tm, width), lambda g, i: (g, i, 0))

    def tab_spec(t):
        if t.shape[1] == 1:
            return pl.BlockSpec((3, 1, t.shape[2]), lambda g, i: (0, 0, 0))
        return pl.BlockSpec((3, tm, t.shape[2]), lambda g, i: (0, i, 0))

    widths = (ATTN_W, ATTN_W, IDX_QW, IDX_PAD_W - IDX_QW)
    return pl.pallas_call(
        _attn_prep_kernel,
        grid=(g, r // tm),
        in_specs=[tok(ATTN_IN_W), tok(IDX_PAD_W), pl.BlockSpec(qk_norm_w.shape, lambda g, i: (0, 0)),
                  tab_spec(tab_attn), tab_spec(tab_idx)],
        out_specs=[tok(w) for w in widths],
        out_shape=[jax.ShapeDtypeStruct((g, r, w), F32) for w in widths],
        compiler_params=_cparams("parallel", "parallel"),
        name="attn_prep",
    )(pa, pi, qk_norm_w, tab_attn, tab_idx)


def _sortable_key(score):
    bits = pltpu.bitcast(score + 0.0, I32)
    return bits ^ ((bits >> 31) & 0x7FFFFFFF)


def _kth_largest_key(count_ge, k_sel, shape):
    t = jnp.where(count_ge(jnp.zeros(shape, I32)) >= k_sel, 0, INT_MIN).astype(I32)

    def body(i, t):
        cand = t | jnp.left_shift(jnp.int32(1), 30 - i)
        return jnp.where(count_ge(cand) >= k_sel, cand, t)

    return lax.fori_loop(0, 31, body, t)


def _tie_index_limit(count_eq_below, need, n_bits, shape):
    def body(i, j):
        cand = j | jnp.left_shift(jnp.int32(1), n_bits - 1 - i)
        return jnp.where(count_eq_below(cand) < need, cand, j)

    return lax.fori_loop(0, n_bits, body, jnp.zeros(shape, I32))


def _dsa_prompt_kernel(tq, tk, k_sel, idx_bits, kidx_ref, k_ref, vt_ref, qit_ref, wt_ref, qt_ref, o_ref,
                       ikey_ref, thr_ref, m_ref, l_ref, acc_ref, s_ref, p_ref, alpha_ref):
    qb = pl.program_id(1)
    n_chunks = ((qb + 1) * tq + tk - 1) // tk
    qpos = qb * tq + lax.broadcasted_iota(I32, (tk, tq), 1)
    row = lax.broadcasted_iota(I32, (tk, tq), 0)
    w = wt_ref[...] * (IDX_QW ** -0.5)

    def score_chunk(c, carry):
        rows = pl.ds(pl.multiple_of(c * tk, tk), tk)
        dots = jnp.dot(kidx_ref[rows, :], qit_ref[...], preferred_element_type=F32)
        sc = jnp.zeros((tk, tq), F32)
        for h in range(IDX_H):
            sc = sc + jnp.maximum(dots[:, h * tq:(h + 1) * tq], 0.0) * w[h:h + 1, :]
        sc = jnp.where(c * tk + row <= qpos, sc, -jnp.inf)
        ikey_ref[rows, :] = _sortable_key(sc)
        return carry

    lax.fori_loop(0, n_chunks, score_chunk, 0)

    part = 32

    def fold(x, op):
        return op(x.reshape(tk // part, part, tq), axis=0)

    def count(pred):
        def body(c, acc):
            rows = pl.ds(pl.multiple_of(c * tk, tk), tk)
            hit = pred(ikey_ref[rows, :], c * tk + row)
            return acc + fold(jnp.where(hit, 1.0, 0.0), jnp.sum)
        acc = lax.fori_loop(0, n_chunks, body, jnp.zeros((part, tq), F32))
        return jnp.sum(acc, axis=0, keepdims=True)

    thr = _kth_largest_key(lambda cand: count(lambda key, pos: key >= cand), k_sel, (1, tq))
    need = k_sel - count(lambda key, pos: key > thr)
    n_eq = count(lambda key, pos: key == thr)
    thr_ref[0:1, :] = thr
    thr_ref[1:2, :] = jnp.full((1, tq), 2 ** 30, I32)

    @pl.when(jnp.max(n_eq - need) > 0.0)
    def _():
        thr_ref[1:2, :] = _tie_index_limit(
            lambda cand: count(lambda key, pos: (key == thr) & (pos < cand)), need, idx_bits, (1, tq))

    m_ref[...] = jnp.full(m_ref.shape, -jnp.inf, F32)
    l_ref[...] = jnp.zeros(l_ref.shape, F32)
    acc_ref[...] = jnp.zeros(acc_ref.shape, F32)
    tie_j = thr_ref[1:2, :]

    def attend_chunk(c, carry):
        rows = pl.ds(pl.multiple_of(c * tk, tk), tk)
        key = ikey_ref[rows, :]
        pos = c * tk + row
        sel = ((key > thr) | ((key == thr) & (pos <= tie_j))) & (pos <= qpos)
        bias = jnp.where(sel, 0.0, -jnp.inf)
        for j in range(ATTN_H // 2):
            s_ref[:, 2 * j * tq:2 * (j + 1) * tq] = jnp.dot(
                k_ref[rows, j * LANES:(j + 1) * LANES], qt_ref[j], preferred_element_type=F32)
        for h in range(ATTN_H):
            cols = slice(h * tq, (h + 1) * tq)
            s = s_ref[:, cols] + bias
            m_old = m_ref[h]
            m_new = jnp.maximum(m_old, jnp.max(fold(s, jnp.max), axis=0, keepdims=True))
            m_safe = jnp.where(m_new == -jnp.inf, 0.0, m_new)
            alpha = jnp.exp2(m_old - m_safe)
            p = jnp.exp2(s - m_safe[0:1, :])
            p_ref[:, cols] = p.astype(BF16)
            l_ref[h] = alpha * l_ref[h] + jnp.sum(fold(p, jnp.sum), axis=0, keepdims=True)
            alpha_ref[h] = alpha
            m_ref[h] = m_new
        for h in range(ATTN_H):
            hs = slice(h * HEAD_DIM, (h + 1) * HEAD_DIM)
            acc_ref[hs, :] = alpha_ref[h][0:1, :] * acc_ref[hs, :] + jnp.dot(
                vt_ref[c, hs, :], p_ref[:, h * tq:(h + 1) * tq], preferred_element_type=F32)
        return carry

    lax.fori_loop(0, n_chunks, attend_chunk, 0)
    for h in range(ATTN_H):
        hs = slice(h * HEAD_DIM, (h + 1) * HEAD_DIM)
        acc_ref[hs, :] = acc_ref[hs, :] / l_ref[h][0:1, :]
    o_ref[...] = acc_ref[...].T


def dsa_prompt(q, k, v, qidx, kidx, widx, tq, tk):
    b, t, _ = q.shape
    nqb, nck = t // tq, t // tk
    k_sel = min(TOPK_MAX, t // 4)
    eye2 = jnp.eye(2, dtype=F32) * (HEAD_DIM ** -0.5 * LOG2_E)
    q_t = q.reshape(b, nqb, tq, ATTN_H // 2, 2, HEAD_DIM).transpose(0, 1, 3, 4, 5, 2)
    q_t = (q_t[:, :, :, :, None] * eye2[None, None, None, :, :, None, None])
    q_t = q_t.transpose(0, 1, 2, 4, 5, 3, 6).reshape(b, nqb, ATTN_H // 2, LANES, 2 * tq).astype(BF16)
    v_t = v.reshape(b, nck, tk, ATTN_W).transpose(0, 1, 3, 2).astype(BF16)
    qi_t = qidx.reshape(b, nqb, tq, IDX_H, IDX_D).transpose(0, 1, 4, 3, 2).reshape(
        b, nqb, IDX_D, IDX_H * tq).astype(BF16)
    w_t = widx.reshape(b, nqb, tq, IDX_H).transpose(0, 1, 3, 2)
    once = dict(pipeline_mode=pl.Buffered(1))
    return pl.pallas_call(
        functools.partial(_dsa_prompt_kernel, tq, tk, k_sel, (t - 1).bit_length()),
        grid=(b, nqb),
        in_specs=[
            pl.BlockSpec((None, t, IDX_D), lambda b, i: (b, 0, 0), **once),
            pl.BlockSpec((None, t, ATTN_W), lambda b, i: (b, 0, 0), **once),
            pl.BlockSpec((None, nck, ATTN_W, tk), lambda b, i: (b, 0, 0, 0), **once),
            pl.BlockSpec((None, None, IDX_D, IDX_H * tq), lambda b, i: (b, i, 0, 0)),
            pl.BlockSpec((None, None, IDX_H, tq), lambda b, i: (b, i, 0, 0)),
            pl.BlockSpec((None, None, ATTN_H // 2, LANES, 2 * tq), lambda b, i: (b, i, 0, 0, 0)),
        ],
        out_specs=pl.BlockSpec((None, tq, ATTN_W), lambda b, i: (b, i, 0)),
        out_shape=jax.ShapeDtypeStruct((b, t, ATTN_W), F32),
        scratch_shapes=[
            pltpu.VMEM((t, tq), I32),
            pltpu.VMEM((8, tq), I32),
            pltpu.VMEM((ATTN_H, 8, tq), F32),
            pltpu.VMEM((ATTN_H, 8, tq), F32),
            pltpu.VMEM((ATTN_W, tq), F32),
            pltpu.VMEM((tk, ATTN_H * tq), F32),
            pltpu.VMEM((tk, ATTN_H * tq), BF16),
            pltpu.VMEM((ATTN_H, 8, tq), F32),
        ],
        compiler_params=_cparams("parallel", "arbitrary"),
        name="dsa_prompt",
    )(kidx.astype(BF16), k.astype(BF16), v_t, qi_t, w_t, q_t)


_NT = (((1,), (1,)), ((), ()))


def _page_specs(n_pages, layer, block):
    def spec(j):
        return pl.BlockSpec((None, None) + block,
                            lambda b, pt: (layer, pt[b * n_pages + j]) + (0,) * len(block))
    return [spec(j) for j in range(n_pages)]


def _sample_scores_kernel(n_pages, pt_ref, *refs):
    pages, (qi_ref, w_ref, o_ref) = refs[:n_pages], refs[n_pages:]
    qi = qi_ref[...].astype(BF16)
    w = w_ref[...] * (IDX_QW ** -0.5)
    for j in range(n_pages):
        dots = jnp.dot(qi, pages[j][...].astype(BF16), preferred_element_type=F32)
        o_ref[:, j * PAGE_SIZE:(j + 1) * PAGE_SIZE] = jnp.sum(jnp.maximum(dots, 0.0) * w, axis=0, keepdims=True)


def sample_scores(layer, page_table_flat, cache_kidx_t, qidx, widx, n_pages):
    nb = qidx.shape[0]
    return pl.pallas_call(
        functools.partial(_sample_scores_kernel, n_pages),
        grid_spec=pltpu.PrefetchScalarGridSpec(
            num_scalar_prefetch=1,
            grid=(nb,),
            in_specs=_page_specs(n_pages, layer, (IDX_D, PAGE_SIZE)) + [
                pl.BlockSpec((None, IDX_H, IDX_D), lambda b, pt: (b, 0, 0)),
                pl.BlockSpec((None, IDX_H, 1), lambda b, pt: (b, 0, 0)),
            ],
            out_specs=pl.BlockSpec((None, 1, n_pages * PAGE_SIZE), lambda b, pt: (b, 0, 0)),
        ),
        out_shape=jax.ShapeDtypeStruct((nb, 1, n_pages * PAGE_SIZE), F32),
        compiler_params=_cparams("parallel"),
        name="sample_scores",
    )(page_table_flat, *([cache_kidx_t] * n_pages), qidx, widx)


def _sample_select_kernel(k_sel, idx_bits, sc_ref, qi_ref, kt_ref, w_ref, bias_ref, bias_self_ref):
    nb, nk = sc_ref.shape
    r = lax.broadcasted_iota(I32, (IDX_QW, LANES), 0) // IDX_D
    c = lax.broadcasted_iota(I32, (IDX_QW, LANES), 1)
    seg = jnp.where(r == c, 1.0, 0.0).astype(BF16)
    prod = qi_ref[...].astype(BF16).astype(F32) * kt_ref[...].astype(BF16).astype(F32)
    dots = _mm_lhs2(prod, seg)
    self_sc = jnp.sum(jnp.maximum(dots, 0.0) * (w_ref[...] * (IDX_QW ** -0.5)), axis=1, keepdims=True)
    key_self = _sortable_key(self_sc)
    keys = _sortable_key(sc_ref[...])
    pos = lax.broadcasted_iota(I32, (nb, nk), 1)

    def count(pred):
        past = jnp.sum(jnp.where(pred(keys, pos), 1.0, 0.0), axis=1, keepdims=True)
        return past + jnp.where(pred(key_self, nk), 1.0, 0.0)

    thr = _kth_largest_key(lambda cand: count(lambda key, p: key >= cand), k_sel, (nb, 1))
    need = k_sel - count(lambda key, p: key > thr)
    tie_j = _tie_index_limit(
        lambda cand: count(lambda key, p: (key == thr) & (p < cand)), need, idx_bits, (nb, 1))
    chosen = lambda key, p: (key > thr) | ((key == thr) & (p <= tie_j))
    bias_ref[...] = jnp.where(chosen(keys, pos), 0.0, -jnp.inf)
    bias_self_ref[...] = jnp.broadcast_to(jnp.where(chosen(key_self, nk), 0.0, -jnp.inf), (nb, LANES))


def sample_select(scores, qidx, kidx_tiled, w_pad):
    nb, nk = scores.shape
    k_sel = min(TOPK_MAX, (nk + 1) // 4)
    return pl.pallas_call(
        functools.partial(_sample_select_kernel, k_sel, nk.bit_length()),
        out_shape=[jax.ShapeDtypeStruct((nb, nk), F32), jax.ShapeDtypeStruct((nb, LANES), F32)],
        compiler_params=pltpu.CompilerParams(vmem_limit_bytes=VMEM_LIMIT),
        name="sample_select",
    )(scores, qidx, kidx_tiled, w_pad)


def _sample_attend_kernel(n_pages, pt_ref, *refs):
    k_pages, v_pages = refs[:n_pages], refs[n_pages:2 * n_pages]
    bias_ref, q_ref, kn_ref, vn_ref, bself_ref, o_ref, s_ref = refs[2 * n_pages:]
    own = (lax.broadcasted_iota(I32, (ATTN_H, ATTN_W), 1) // HEAD_DIM
           == lax.broadcasted_iota(I32, (ATTN_H, ATTN_W), 0))
    q_bd = jnp.where(own, q_ref[...], 0.0) * (HEAD_DIM ** -0.5)
    q_bf = q_bd.astype(BF16)
    for j in range(n_pages):
        cols = slice(j * PAGE_SIZE, (j + 1) * PAGE_SIZE)
        k_t = k_pages[j][...].reshape(ATTN_W, PAGE_SIZE).astype(BF16)
        s_ref[:, cols] = jnp.dot(q_bf, k_t, preferred_element_type=F32) + bias_ref[:, cols]
    s = s_ref[...]
    s_self = jnp.sum(q_bd * kn_ref[...], axis=1, keepdims=True) + bself_ref[:, 0:1]
    m = jnp.maximum(jnp.max(s, axis=1, keepdims=True), s_self)
    p = jnp.exp(s - m)
    p_self = jnp.exp(s_self - m)
    denom = jnp.sum(p, axis=1, keepdims=True) + p_self
    acc = p_self * vn_ref[...]
    for j in range(n_pages):
        v_t = v_pages[j][...].reshape(ATTN_W, PAGE_SIZE).astype(BF16)
        acc = acc + lax.dot_general(p[:, j * PAGE_SIZE:(j + 1) * PAGE_SIZE].astype(BF16), v_t, _NT,
                                    preferred_element_type=F32)
    o_ref[...] = jnp.sum(jnp.where(own, acc / denom, 0.0), axis=0, keepdims=True)


def sample_attend(layer, page_table_flat, cache_k_t, cache_v_t, bias, bias_self, q, k_new, v_new, n_pages):
    nb = q.shape[0]
    past = n_pages * PAGE_SIZE
    row = pl.BlockSpec((None, 1, ATTN_W), lambda b, pt: (b, 0, 0))
    pages = _page_specs(n_pages, layer, (ATTN_H, HEAD_DIM, PAGE_SIZE))
    return pl.pallas_call(
        functools.partial(_sample_attend_kernel, n_pages),
        grid_spec=pltpu.PrefetchScalarGridSpec(
            num_scalar_prefetch=1,
            grid=(nb,),
            in_specs=pages + pages + [
                pl.BlockSpec((None, 1, past), lambda b, pt: (b, 0, 0)),
                row, row, row,
                pl.BlockSpec((None, 1, LANES), lambda b, pt: (b, 0, 0))],
            out_specs=row,
            scratch_shapes=[pltpu.VMEM((ATTN_H, past), F32)],
        ),
        out_shape=jax.ShapeDtypeStruct((nb, 1, ATTN_W), F32),
        compiler_params=_cparams("parallel"),
        name="sample_attend",
    )(page_table_flat, *([cache_k_t] * n_pages), *([cache_v_t] * n_pages), bias, q, k_new, v_new, bias_self)


def _ffn_kernel(routed, x_ref, nw_ref, mod_ref, router_ref, wg_ref, wu_ref, wd_ref, y_ref,
                h_ref, acc_ref, comb_ref):
    d = D_MODEL
    j = pl.program_id(2)

    @pl.when(j == 0)
    def _():
        h = _modulated_norm(x_ref[...], nw_ref[...], mod_ref[:, 3 * d:4 * d], mod_ref[:, 4 * d:5 * d])
        h_ref[...] = h.astype(BF16)
        acc_ref[...] = jnp.zeros(acc_ref.shape, F32)
        if routed:
            lane = lax.broadcasted_iota(I32, comb_ref.shape, 1).astype(F32)
            logits = jnp.where(lane < N_EXPERTS, _mm3(h, router_ref[...]), -jnp.inf)
            m1 = jnp.max(logits, axis=1, keepdims=True)
            i1 = jnp.min(jnp.where(logits == m1, lane, float(LANES)), axis=1, keepdims=True)
            rest = jnp.where(lane == i1, -jnp.inf, logits)
            m2 = jnp.max(rest, axis=1, keepdims=True)
            i2 = jnp.min(jnp.where(rest == m2, lane, float(LANES)), axis=1, keepdims=True)
            e2 = jnp.exp(m2 - m1)
            comb_ref[...] = (jnp.where(lane == i1, 1.0 / (1.0 + e2), 0.0)
                             + jnp.where(lane == i2, e2 / (1.0 + e2), 0.0))

    h = h_ref[...]
    t = _silu(jnp.dot(h, wg_ref[...], preferred_element_type=F32)) * jnp.dot(
        h, wu_ref[...], preferred_element_type=F32)
    y = jnp.dot(t.astype(BF16), wd_ref[...], preferred_element_type=F32)
    if routed:
        lane = lax.broadcasted_iota(I32, comb_ref.shape, 1)
        y = y * jnp.sum(jnp.where(lane == j, comb_ref[...], 0.0), axis=1, keepdims=True)
    acc_ref[...] += y

    @pl.when(j == pl.num_programs(2) - 1)
    def _():
        y_ref[...] = x_ref[...] + mod_ref[:, 5 * d:6 * d] * acc_ref[...]


def ffn(x, norm_w, mod, router_pad, wg, wu, wd, tm, routed):
    g, r, d = x.shape
    nj, _, f = wg.shape
    tok = pl.BlockSpec((None, tm, d), lambda g, i, j: (g, i, 0))
    if mod.shape[1] == 1:
        mod_spec = pl.BlockSpec((None, 1, mod.shape[2]), lambda g, i, j: (g, 0, 0))
    else:
        mod_spec = pl.BlockSpec((None, tm, mod.shape[2]), lambda g, i, j: (g, i, 0))
    return pl.pallas_call(
        functools.partial(_ffn_kernel, routed),
        grid=(g, r // tm, nj),
        in_specs=[tok, pl.BlockSpec((1, d), lambda g, i, j: (0, 0)), mod_spec,
                  pl.BlockSpec(router_pad.shape, lambda g, i, j: (0, 0)),
                  pl.BlockSpec((None, d, f), lambda g, i, j: (j, 0, 0)),
                  pl.BlockSpec((None, d, f), lambda g, i, j: (j, 0, 0)),
                  pl.BlockSpec((None, f, d), lambda g, i, j: (j, 0, 0))],
        out_specs=tok,
        out_shape=jax.ShapeDtypeStruct(x.shape, F32),
        scratch_shapes=[pltpu.VMEM((tm, d), BF16), pltpu.VMEM((tm, d), F32), pltpu.VMEM((tm, LANES), F32)],
        compiler_params=_cparams("parallel", "parallel", "arbitrary"),
        name="ffn_routed" if routed else "ffn_dense",
    )(x, norm_w.reshape(1, d), mod, router_pad, wg, wu, wd)


def _layer_params(l, p):
    z = jnp.zeros((LORA_W, RWKV_W), F32)
    w_in = jnp.pad(p["w_in"][l], ((0, 0), (0, SHIFT_W + ATTN_IN_W + IDX_PAD_W - IN_W))).astype(BF16)
    j = l // 2
    if l % 2 == 0:
        halves = lambda w: jnp.stack(jnp.split(w, w.shape[1] // D_FF_TILE, axis=1)).astype(BF16)
        wg, wu = halves(p["ffn_w_gate"][j]), halves(p["ffn_w_up"][j])
        wd = jnp.stack(jnp.split(p["ffn_w_down"][j], wg.shape[0], axis=0)).astype(BF16)
        router = jnp.zeros((8, LANES), F32)
    else:
        wg, wu, wd = (p[n][j].astype(BF16) for n in ("exp_w_gate", "exp_w_up", "exp_w_down"))
        router = jnp.pad(p["router_w"][j], ((0, 0), (0, LANES - N_EXPERTS)))
    return dict(
        norm1_w=p["norm1_w"][l], norm2_w=p["norm2_w"][l], w_in=w_in, mu=p["mu_shift"][l][None, :],
        vecs=jnp.stack([p["rwkv_w0"][l], p["rwkv_a0"][l], p["rwkv_k_k"][l], p["rwkv_k_a"][l],
                        p["rwkv_r_k"][l].reshape(RWKV_W)] + [jnp.zeros((RWKV_W,), F32)] * 3),
        w2=z.at[0:W_LORA].set(p["rwkv_w2"][l]),
        a2=z.at[W_LORA:W_LORA + A_LORA].set(p["rwkv_a2"][l]),
        g2=z.at[W_LORA + A_LORA:].set(p["rwkv_g2"][l]),
        lnx=jnp.stack([p["lnx_w"][l], p["lnx_b"][l]]),
        qkw=jnp.stack([jnp.tile(p["q_norm_w"][l], ATTN_H), jnp.tile(p["k_norm_w"][l], ATTN_H)]),
        w_out=p["w_out"][l].astype(BF16), router=router, wg=wg, wu=wu, wd=wd, routed=(l % 2 == 1))


def _trunk_layer(x, mod, lp, shift_prev, s0, tabs, seq_mode, tm, attend):
    g, r, _ = x.shape
    pr, pa, pi = proj_in(x, lp["norm1_w"], mod, lp["w_in"], tm)
    dr, dd, k2, vr, kk, bb_, gate, bonus, kr = rwkv_prep(pr, shift_prev, lp["mu"], lp["vecs"], lp["w2"],
                                                         lp["a2"], lp["g2"], tm, seq_mode)
    row_operands = (dr, dd, k2, kk, bb_, kr)
    if seq_mode:
        v_t = vr.reshape(g, r, RWKV_H, HEAD_DIM).transpose(0, 2, 3, 1)
        o_tiles, s_fin = rwkv_scan(row_operands, v_t, _pack_state(s0), g, LANES)
        o_scan = _readout_rows(o_tiles, r)
    else:
        o_tiles, s_fin = rwkv_scan([a.reshape(r, 1, RWKV_W) for a in row_operands],
                                   vr.reshape(r, RWKV_H, HEAD_DIM, 1), _pack_state(s0), 8, 1)
        o_scan = _readout_rows(o_tiles, 1).reshape(g, r, RWKV_W)
    q, k, qi, ki = attn_prep(pa, pi, lp["qkw"], tabs[0], tabs[1], tm)
    v = pa[..., 2 * ATTN_W:]
    kidx = ki[..., :IDX_D]
    widx = ki[..., IDX_D:IDX_D + IDX_H]
    o_attn = attend(q, k, v, qi, kidx, widx)
    x = proj_out(x, o_scan, bonus, gate, o_attn, lp["lnx"], mod, lp["w_out"], tm)
    x = ffn(x, lp["norm2_w"], mod, lp["router"], lp["wg"], lp["wu"], lp["wd"], min(512, r), lp["routed"])
    return x, (k, v, kidx, _unpack_state(s_fin), pr[:, -1] if seq_mode else pr[0])


def kernel(x_prompt, x_sample, c_prompt, c_sample, cache_k, cache_v, cache_kidx, state_wkv, state_shift, page_table, norm1_w, norm2_w, w_ada, b_ada, w_in, mu_shift, rwkv_w0, rwkv_w2, rwkv_a0, rwkv_a2, rwkv_g2, rwkv_k_k, rwkv_k_a, rwkv_r_k, lnx_w, lnx_b, q_norm_w, k_norm_w, w_out, ffn_w_gate, ffn_w_up, ffn_w_down, router_w, exp_w_gate, exp_w_up, exp_w_down):
    params = dict(norm1_w=norm1_w, norm2_w=norm2_w, w_in=w_in, mu_shift=mu_shift, rwkv_w0=rwkv_w0,
                  rwkv_w2=rwkv_w2, rwkv_a0=rwkv_a0, rwkv_a2=rwkv_a2, rwkv_g2=rwkv_g2, rwkv_k_k=rwkv_k_k,
                  rwkv_k_a=rwkv_k_a, rwkv_r_k=rwkv_r_k, lnx_w=lnx_w, lnx_b=lnx_b, q_norm_w=q_norm_w,
                  k_norm_w=k_norm_w, w_out=w_out, ffn_w_gate=ffn_w_gate, ffn_w_up=ffn_w_up,
                  ffn_w_down=ffn_w_down, router_w=router_w, exp_w_gate=exp_w_gate, exp_w_up=exp_w_up,
                  exp_w_down=exp_w_down)
    depth = w_in.shape[0]
    bp, t, d = x_prompt.shape
    nb = x_sample.shape[0]
    n_pages = page_table.shape[1]
    past = n_pages * PAGE_SIZE
    pt_flat = page_table.reshape(-1).astype(I32)
    cache_k_t = jnp.transpose(cache_k, (0, 1, 3, 4, 2))
    cache_v_t = jnp.transpose(cache_v, (0, 1, 3, 4, 2))
    cache_kidx_t = jnp.transpose(cache_kidx, (0, 1, 3, 2))

    n_mod = bp + nb
    c_all = jnp.pad(jnp.concatenate([c_prompt, c_sample]), ((0, -n_mod % 8), (0, 0)))
    mod = ada_modulation(c_all, w_ada, b_ada)
    mod_p = mod[:, :bp, None, :]
    mod_s = mod[:, None, bp:n_mod, :]

    def tables(pos):
        return (rope_tables(pos, ATTN_W, HEAD_DIM, ROT_DIM, ATTN_W),
                rope_tables(pos, IDX_PAD_W, IDX_D, IDX_ROT_DIM, IDX_QW + IDX_D))

    tabs_p = tables(jnp.arange(t, dtype=I32))
    tabs_s = tables(jnp.full((1,), past, I32))
    tm_p = min(256, t)
    tq, tk = min(128, t), min(512, t)

    xp, xs = x_prompt, x_sample.reshape(1, nb, d)
    st_p, st_s = [], []
    for l in range(depth):
        lp = _layer_params(l, params)
        attend_p = lambda q, k, v, qi, ki, wi: dsa_prompt(q, k, v, qi, ki, wi, tq, tk)

        def attend_s(q, k, v, qi, ki, wi, l=l):
            rows = lambda a: a.reshape(nb, 1, ATTN_W)
            qi_h = qi.reshape(nb, IDX_H, IDX_D)
            scores = sample_scores(l, pt_flat, cache_kidx_t, qi_h, wi.reshape(nb, IDX_H, 1), n_pages)
            w_pad = jnp.pad(wi.reshape(nb, IDX_H), ((0, 0), (0, LANES - IDX_H)))
            bias, bias_self = sample_select(scores.reshape(nb, past), qi.reshape(nb, IDX_QW),
                                            jnp.tile(ki.reshape(nb, IDX_D), (1, IDX_H)), w_pad)
            o = sample_attend(l, pt_flat, cache_k_t, cache_v_t, bias.reshape(nb, 1, past),
                              bias_self.reshape(nb, 1, LANES), rows(q), rows(k), rows(v), n_pages)
            return o.reshape(1, nb, ATTN_W)

        xp, sp = _trunk_layer(xp, mod_p[l], lp, jnp.zeros((bp, 1, SHIFT_W), F32),
                              jnp.zeros((bp, RWKV_H, HEAD_DIM, HEAD_DIM), F32), tabs_p, True, tm_p, attend_p)
        xs, ss = _trunk_layer(xs, mod_s[l], lp, state_shift[l][None], state_wkv[l], tabs_s, False, nb, attend_s)
        st_p.append(sp)
        st_s.append(ss)

    def stack(states, i, shape):
        return jnp.stack([s[i] for s in states]).reshape(shape)

    return (xp, xs.reshape(nb, 1, d),
            stack(st_p, 0, (depth, bp, t, ATTN_H, HEAD_DIM)), stack(st_p, 1, (depth, bp, t, ATTN_H, HEAD_DIM)),
            stack(st_p, 2, (depth, bp, t, IDX_D)), stack(st_p, 3, (depth, bp, RWKV_H, HEAD_DIM, HEAD_DIM)),
            stack(st_p, 4, (depth, bp, SHIFT_W)),
            stack(st_s, 0, (depth, nb, 1, ATTN_H, HEAD_DIM)), stack(st_s, 1, (depth, nb, 1, ATTN_H, HEAD_DIM)),
            stack(st_s, 2, (depth, nb, 1, IDX_D)), stack(st_s, 3, (depth, nb, RWKV_H, HEAD_DIM, HEAD_DIM)),
            stack(st_s, 4, (depth, nb, SHIFT_W)))
```

```python
import functools

import jax
import jax.numpy as jnp
from jax import lax
from jax.experimental import pallas as pl
from jax.experimental.pallas import tpu as pltpu

F32 = jnp.float32
BF16 = jnp.bfloat16
I32 = jnp.int32

D_MODEL = 1024
HEAD_DIM = 64
RWKV_W = 512
RWKV_H = 8
ATTN_W = 512
ATTN_H = 8
W_LORA = 64
A_LORA = 64
G_LORA = 128
LORA_W = W_LORA + A_LORA + G_LORA
IDX_H = 8
IDX_D = 32
IDX_QW = IDX_H * IDX_D
TOPK_MAX = 256
ROT_DIM = 16
IDX_ROT_DIM = 8
ROPE_THETA = 500000.0
PAGE_SIZE = 128
D_FF_TILE = 1408
N_EXPERTS = 8
NORM_EPS = 1e-6
LNX_EPS = 64e-5
SHIFT_W = 3 * RWKV_W + LORA_W
ATTN_IN_W = ATTN_W + 2 * ATTN_W
IDX_IN_W = IDX_QW + IDX_D + IDX_H
IDX_PAD_W = 384
IN_W = SHIFT_W + ATTN_IN_W + IDX_IN_W

LANES = 128
VMEM_LIMIT = 56 * 1024 * 1024
INT_MIN = -(2 ** 31)
LOG2_E = 1.4426950408889634


def _cparams(*sem):
    return pltpu.CompilerParams(dimension_semantics=sem, vmem_limit_bytes=VMEM_LIMIT)


def _mm(a, b):
    return jnp.dot(a.astype(BF16), b.astype(BF16), preferred_element_type=F32)


def _split2(a):
    hi = a.astype(BF16)
    lo = (a - hi.astype(F32)).astype(BF16)
    return hi, lo


def _mm_lhs2(a, b_bf16):
    hi, lo = _split2(a)
    return (jnp.dot(hi, b_bf16, preferred_element_type=F32)
            + jnp.dot(lo, b_bf16, preferred_element_type=F32))


def _mm3(a, b):
    a_hi, a_lo = _split2(a)
    b_hi, b_lo = _split2(b)
    return (jnp.dot(a_hi, b_hi, preferred_element_type=F32)
            + jnp.dot(a_lo, b_hi, preferred_element_type=F32)
            + jnp.dot(a_hi, b_lo, preferred_element_type=F32))


def _silu(x):
    return x * jax.nn.sigmoid(x)


def _seg_ones(width, seg, scale=1.0):
    r = lax.broadcasted_iota(I32, (width, width), 0) // seg
    c = lax.broadcasted_iota(I32, (width, width), 1) // seg
    return jnp.where(r == c, scale, 0.0).astype(BF16)


def _ada_kernel(c_ref, w_ref, b_ref, o_ref):
    o_ref[...] = _mm3(_silu(c_ref[...]), w_ref[...]) + b_ref[...]


def ada_modulation(c_all, w_ada, b_ada):
    m, d = c_all.shape
    nl, _, n = w_ada.shape
    tn = 1536
    return pl.pallas_call(
        _ada_kernel,
        grid=(nl, n // tn),
        in_specs=[
            pl.BlockSpec((m, d), lambda l, j: (0, 0)),
            pl.BlockSpec((None, d, tn), lambda l, j: (l, 0, j)),
            pl.BlockSpec((None, 1, tn), lambda l, j: (l, 0, j)),
        ],
        out_specs=pl.BlockSpec((None, m, tn), lambda l, j: (l, 0, j)),
        out_shape=jax.ShapeDtypeStruct((nl, m, n), F32),
        compiler_params=_cparams("parallel", "parallel"),
        name="ada_modulation",
    )(c_all, w_ada, b_ada.reshape(nl, 1, n))


def _modulated_norm(x, norm_w, shift, scale):
    y = x * lax.rsqrt(jnp.mean(x * x, axis=-1, keepdims=True) + NORM_EPS)
    return (y * norm_w) * (1.0 + scale) + shift


def _proj_in_kernel(x_ref, nw_ref, mod_ref, w_ref, pr_ref, pa_ref, pi_ref):
    d = D_MODEL
    h = _modulated_norm(x_ref[...], nw_ref[...], mod_ref[:, 0:d], mod_ref[:, d:2 * d])
    p = jnp.dot(h.astype(BF16), w_ref[...], preferred_element_type=F32)
    pr_ref[...] = p[:, :SHIFT_W]
    pa_ref[...] = p[:, SHIFT_W:SHIFT_W + ATTN_IN_W]
    pi_ref[...] = p[:, SHIFT_W + ATTN_IN_W:]


def _mod_spec(mod, tm):
    if mod.shape[1] == 1:
        return pl.BlockSpec((None, 1, mod.shape[2]), lambda g, i: (g, 0, 0))
    return pl.BlockSpec((None, tm, mod.shape[2]), lambda g, i: (g, i, 0))


def proj_in(x, norm_w, mod, w_in_bf16, tm):
    g, r, d = x.shape
    n = w_in_bf16.shape[1]
    widths = (SHIFT_W, ATTN_IN_W, n - SHIFT_W - ATTN_IN_W)
    return pl.pallas_call(
        _proj_in_kernel,
        grid=(g, r // tm),
        in_specs=[
            pl.BlockSpec((None, tm, d), lambda g, i: (g, i, 0)),
            pl.BlockSpec((1, d), lambda g, i: (0, 0)),
            _mod_spec(mod, tm),
            pl.BlockSpec((d, n), lambda g, i: (0, 0)),
        ],
        out_specs=[pl.BlockSpec((None, tm, w), lambda g, i: (g, i, 0)) for w in widths],
        out_shape=[jax.ShapeDtypeStruct((g, r, w), F32) for w in widths],
        compiler_params=_cparams("parallel", "parallel"),
        name="proj_in",
    )(x, norm_w.reshape(1, d), mod, w_in_bf16)


def _rwkv_prep_kernel(seq_mode, p_ref, halo_ref, sp_ref, mu_ref, vec_ref, w2_ref, a2_ref, g2_ref,
                      dr_ref, d_ref, k_ref, v_ref, kk_ref, b_ref, g_ref, bonus_ref, kr_ref):
    p = p_ref[...]
    if seq_mode:
        first = jnp.where(pl.program_id(1) == 0, sp_ref[...], halo_ref[7:8, :])
        row = lax.broadcasted_iota(I32, p.shape, 0)
        prev = jnp.where(row == 0, first, pltpu.roll(p, 1, axis=0))
    else:
        prev = sp_ref[...]
    xs = p + mu_ref[...] * (prev - p)
    w = RWKV_W
    r = xs[:, 0:w]
    k = xs[:, w:2 * w]
    v = xs[:, 2 * w:3 * w]
    lo = xs[:, 3 * w:]
    lane = lax.broadcasted_iota(I32, lo.shape, 1)
    act = jnp.where(lane < W_LORA, jnp.tanh(lo),
                    jnp.where(lane < W_LORA + A_LORA, lo, jax.nn.sigmoid(lo)))
    w0, a0, k_k, k_a, r_k = (vec_ref[i:i + 1, :] for i in range(5))
    z = -(w0 + _mm3(act, w2_ref[...]))
    softplus = jnp.maximum(z, 0.0) + jnp.log(1.0 + jnp.exp(-jnp.abs(z)))
    decay = jnp.exp(-jnp.exp(-softplus - 0.5))
    a = jax.nn.sigmoid(a0 + _mm3(act, a2_ref[...]))
    g = _mm3(act, g2_ref[...])
    seg = _seg_ones(w, HEAD_DIM)
    kk = k * k_k
    kk = kk * lax.rsqrt(jnp.maximum(_mm_lhs2(kk * kk, seg), 1e-24))
    k2 = k * (1.0 + (a - 1.0) * k_a)
    b = kk * a
    dr_ref[...] = decay * r - _mm_lhs2(b * r, seg) * kk
    d_ref[...] = decay
    k_ref[...] = k2
    v_ref[...] = v
    kk_ref[...] = kk
    b_ref[...] = b
    g_ref[...] = g
    bonus_ref[...] = _mm_lhs2(r * k2 * r_k, seg) * v
    kr_ref[...] = _mm_lhs2(k2 * r, seg)


def rwkv_prep(p, shift_prev, mu, vecs, w2p, a2p, g2p, tm, seq_mode):
    g, r, sw = p.shape
    w = RWKV_W
    tok = lambda width: pl.BlockSpec((None, tm, width), lambda g, i: (g, i, 0))
    full = lambda a: pl.BlockSpec(a.shape, lambda g, i: (0,) * a.ndim)
    if seq_mode:
        halo_spec = pl.BlockSpec((None, 8, sw), lambda g, i: (g, jnp.maximum(i * (tm // 8) - 1, 0), 0))
        sp_spec = pl.BlockSpec((None, 1, sw), lambda g, i: (g, 0, 0))
    else:
        halo_spec = pl.BlockSpec((None, 8, sw), lambda g, i: (g, 0, 0))
        sp_spec = tok(sw)
    return pl.pallas_call(
        functools.partial(_rwkv_prep_kernel, seq_mode),
        grid=(g, r // tm),
        in_specs=[tok(sw), halo_spec, sp_spec, full(mu), full(vecs), full(w2p), full(a2p), full(g2p)],
        out_specs=[tok(w)] * 9,
        out_shape=[jax.ShapeDtypeStruct((g, r, w), F32)] * 9,
        compiler_params=_cparams("parallel", "arbitrary"),
        name="rwkv_prep",
    )(p, p, shift_prev, mu, vecs, w2p, a2p, g2p)


OUT_WINDOW = 64


def _rwkv_scan_kernel(bb_n, tt, dr_ref, d_ref, k_ref, kk_ref, b_ref, kr_ref, vt_ref, s0_ref,
                      ot_ref, st_ref):
    @pl.when(pl.program_id(1) == 0)
    def _():
        st_ref[...] = s0_ref[...]

    ot_ref[...] = jnp.zeros_like(ot_ref)
    lane = lax.broadcasted_iota(I32, (HEAD_DIM, LANES), 1)
    lane_a = lane < HEAD_DIM
    t_lane = lax.broadcasted_iota(I32, (HEAD_DIM, tt), 1)
    pairs = [(bb, j) for bb in range(bb_n) for j in range(RWKV_H // 2)]
    ones_r = lax.broadcasted_iota(I32, (2 * LANES, LANES), 0) % LANES // HEAD_DIM
    ones_c = lax.broadcasted_iota(I32, (2 * LANES, LANES), 1) // HEAD_DIM
    stacked_ones = jnp.where(ones_r == ones_c, 1.0, 0.0).astype(BF16)

    def head_sums(x):
        hi, lo = _split2(x)
        return jnp.dot(jnp.concatenate([hi, lo], axis=1), stacked_ones, preferred_element_type=F32)

    def step(t, rows_of, u):
        hot = t_lane == t
        hot_out = (lane % OUT_WINDOW) == (t % OUT_WINDOW)
        window = t // OUT_WINDOW

        def rows_all(ref):
            return jnp.concatenate(
                [jnp.broadcast_to(rows_of(ref, bb, slice(j * LANES, (j + 1) * LANES))[u:u + 1, :],
                                  (HEAD_DIM, LANES)) for bb, j in pairs], axis=0)

        def value_cols(bb, j):
            v_a = jnp.sum(jnp.where(hot, vt_ref[bb, 2 * j], 0.0), axis=1, keepdims=True)
            v_b = jnp.sum(jnp.where(hot, vt_ref[bb, 2 * j + 1], 0.0), axis=1, keepdims=True)
            return jnp.where(lane_a, v_a, v_b)

        s = jnp.concatenate([st_ref[bb, j] for bb, j in pairs], axis=0)
        sk = head_sums(s * rows_all(kk_ref))
        s_dr = jnp.dot((s * rows_all(dr_ref)).astype(BF16), stacked_ones[:LANES], preferred_element_type=F32)
        vv = jnp.concatenate([value_cols(bb, j) for bb, j in pairs], axis=0)
        s_new = s * rows_all(d_ref) - sk * rows_all(b_ref) + vv * rows_all(k_ref)
        out = s_dr + rows_all(kr_ref) * vv
        for i, (bb, j) in enumerate(pairs):
            rows = slice(i * HEAD_DIM, (i + 1) * HEAD_DIM)
            st_ref[bb, j] = s_new[rows]
            ot_ref[bb, j, window] = jnp.where(hot_out, out[rows], ot_ref[bb, j, window])

    sub = 8
    if tt % sub:
        for t in range(tt):
            step(t, lambda ref, bb, ls: ref[bb, :, ls], t)
    else:
        def group(g, carry):
            base = pl.multiple_of(g * sub, sub)
            for u in range(sub):
                step(base + u, lambda ref, bb, ls: ref[bb, pl.ds(base, sub), ls], u)
            return carry

        lax.fori_loop(0, tt // sub, group, 0)


def rwkv_scan(row_operands, v_t, s0_packed, bb, tt):
    ns, t, w = row_operands[0].shape
    n_win, win_blk = pl.cdiv(t, OUT_WINDOW), pl.cdiv(tt, OUT_WINDOW)
    row_spec = pl.BlockSpec((bb, tt, w), lambda n, i: (n, i, 0))
    vt_spec = pl.BlockSpec((bb, RWKV_H, HEAD_DIM, tt), lambda n, i: (n, 0, 0, i))
    st_spec = pl.BlockSpec((bb, RWKV_H // 2, HEAD_DIM, LANES), lambda n, i: (n, 0, 0, 0))
    ot_spec = pl.BlockSpec((bb, RWKV_H // 2, win_blk, HEAD_DIM, LANES), lambda n, i: (n, 0, i, 0, 0))
    return pl.pallas_call(
        functools.partial(_rwkv_scan_kernel, bb, tt),
        grid=(ns // bb, t // tt),
        in_specs=[row_spec] * len(row_operands) + [vt_spec, st_spec],
        out_specs=[ot_spec, st_spec],
        out_shape=[jax.ShapeDtypeStruct((ns, RWKV_H // 2, n_win, HEAD_DIM, LANES), F32),
                   jax.ShapeDtypeStruct(s0_packed.shape, F32)],
        compiler_params=_cparams("parallel", "arbitrary"),
        name="rwkv_scan",
    )(*row_operands, v_t, s0_packed)


def _readout_rows(o_tiles, t):
    ns, _, n_win = o_tiles.shape[:3]
    o = o_tiles.reshape(ns, RWKV_H // 2, n_win, HEAD_DIM, 2, OUT_WINDOW).transpose(0, 2, 5, 1, 4, 3)
    return o.reshape(ns, n_win * OUT_WINDOW, RWKV_W)[:, :t]


def _pack_state(s):
    ns = s.shape[0]
    return (s.reshape(ns, RWKV_H // 2, 2, HEAD_DIM, HEAD_DIM).transpose(0, 1, 3, 2, 4)
            .reshape(ns, RWKV_H // 2, HEAD_DIM, LANES))


def _unpack_state(s):
    ns = s.shape[0]
    return (s.reshape(ns, RWKV_H // 2, HEAD_DIM, 2, HEAD_DIM).transpose(0, 1, 3, 2, 4)
            .reshape(ns, RWKV_H, HEAD_DIM, HEAD_DIM))


def _rwkv_readout(o, bonus, g, lnx_w, lnx_b):
    avg = _seg_ones(RWKV_W, HEAD_DIM, 1.0 / HEAD_DIM)
    mean = _mm_lhs2(o, avg)
    c = o - mean
    var = _mm_lhs2(c * c, avg)
    return ((c * lax.rsqrt(var + LNX_EPS)) * lnx_w + lnx_b + bonus) * g


def _proj_out_kernel(x_ref, o_ref, bonus_ref, g_ref, oa_ref, ln_ref, mod_ref, w_ref, y_ref):
    d = D_MODEL
    o_rwkv = _rwkv_readout(o_ref[...], bonus_ref[...], g_ref[...], ln_ref[0:1, :], ln_ref[1:2, :])
    mix = jnp.concatenate([o_rwkv, oa_ref[...]], axis=-1).astype(BF16)
    y = jnp.dot(mix, w_ref[...], preferred_element_type=F32)
    y_ref[...] = x_ref[...] + mod_ref[:, 2 * d:3 * d] * y


def proj_out(x, o_scan, bonus, g, o_attn, lnx, mod, w_out_bf16, tm):
    gg, r, d = x.shape
    tok = lambda width: pl.BlockSpec((None, tm, width), lambda g, i: (g, i, 0))
    full = lambda a: pl.BlockSpec(a.shape, lambda g, i: (0,) * a.ndim)
    return pl.pallas_call(
        _proj_out_kernel,
        grid=(gg, r // tm),
        in_specs=[tok(d), tok(RWKV_W), tok(RWKV_W), tok(RWKV_W), tok(ATTN_W), full(lnx),
                  _mod_spec(mod, tm), full(w_out_bf16)],
        out_specs=tok(d),
        out_shape=jax.ShapeDtypeStruct(x.shape, F32),
        compiler_params=_cparams("parallel", "parallel"),
        name="proj_out",
    )(x, o_scan, bonus, g, o_attn, lnx, mod, w_out_bf16)


def rope_tables(pos, width, group, rot_dim, rot_width):
    half = rot_dim // 2
    inv_freq = ROPE_THETA ** (-jnp.arange(half, dtype=F32) / half)
    ang = pos.astype(F32)[:, None] * inv_freq[None, :]
    lane = jnp.arange(width)
    j = lane % group
    rot = (lane < rot_width) & (j < rot_dim)
    cos = jnp.where(rot[None, :], jnp.cos(ang)[:, j % half], 1.0)
    sin = jnp.sin(ang)[:, j % half]
    sin_lo = jnp.where((rot & (j < half))[None, :], -sin, 0.0)
    sin_hi = jnp.where((rot & (j >= half))[None, :], sin, 0.0)
    return jnp.stack([cos, sin_lo, sin_hi]).astype(F32)


def _rope(x, tab_ref, half):
    width = x.shape[-1]
    return (x * tab_ref[0] + pltpu.roll(x, width - half, axis=1) * tab_ref[1]
            + pltpu.roll(x, half, axis=1) * tab_ref[2])


def _attn_prep_kernel(pa_ref, pi_ref, qkw_ref, ta_ref, ti_ref, q_ref, k_ref, qi_ref, ki_ref):
    avg = _seg_ones(ATTN_W, HEAD_DIM, 1.0 / HEAD_DIM)

    def head_norm(x, w):
        return x * lax.rsqrt(_mm_lhs2(x * x, avg) + NORM_EPS) * w

    q = head_norm(pa_ref[:, 0:ATTN_W], qkw_ref[0:1, :])
    k = head_norm(pa_ref[:, ATTN_W:2 * ATTN_W], qkw_ref[1:2, :])
    q_ref[...] = _rope(q, ta_ref, ROT_DIM // 2)
    k_ref[...] = _rope(k, ta_ref, ROT_DIM // 2)
    pi = _rope(pi_ref[...], ti_ref, IDX_ROT_DIM // 2)
    qi_ref[...] = pi[:, 0:IDX_QW]
    ki_ref[...] = pi[:, IDX_QW:]


def attn_prep(pa, pi, qk_norm_w, tab_attn, tab_idx, tm):
    g, r, _ = pa.shape
    tok = lambda width: pl.BlockSpec((None, tm, width), lambda g, i: (g, i, 0))

    def tab_spec(t):
        if t.shape[1] == 1:
            return pl.BlockSpec((3, 1, t.shape[2]), lambda g, i: (0, 0, 0))
        return pl.BlockSpec((3, tm, t.shape[2]), lambda g, i: (0, i, 0))

    widths = (ATTN_W, ATTN_W, IDX_QW, IDX_PAD_W - IDX_QW)
    return pl.pallas_call(
        _attn_prep_kernel,
        grid=(g, r // tm),
        in_specs=[tok(ATTN_IN_W), tok(IDX_PAD_W), pl.BlockSpec(qk_norm_w.shape, lambda g, i: (0, 0)),
                  tab_spec(tab_attn), tab_spec(tab_idx)],
        out_specs=[tok(w) for w in widths],
        out_shape=[jax.ShapeDtypeStruct((g, r, w), F32) for w in widths],
        compiler_params=_cparams("parallel", "parallel"),
        name="attn_prep",
    )(pa, pi, qk_norm_w, tab_attn, tab_idx)


def _sortable_key(score):
    bits = pltpu.bitcast(score + 0.0, I32)
    return bits ^ ((bits >> 31) & 0x7FFFFFFF)


def _kth_largest_key(count_ge, k_sel, shape):
    t = jnp.where(count_ge(jnp.zeros(shape, I32)) >= k_sel, 0, INT_MIN).astype(I32)

    def body(i, t):
        cand = t | jnp.left_shift(jnp.int32(1), 30 - i)
        return jnp.where(count_ge(cand) >= k_sel, cand, t)

    return lax.fori_loop(0, 31, body, t)


def _tie_index_limit(count_eq_below, need, n_bits, shape):
    def body(i, j):
        cand = j | jnp.left_shift(jnp.int32(1), n_bits - 1 - i)
        return jnp.where(count_eq_below(cand) < need, cand, j)

    return lax.fori_loop(0, n_bits, body, jnp.zeros(shape, I32))


def _dsa_prompt_kernel(tq, tk, k_sel, idx_bits, kidx_ref, k_ref, vt_ref, qit_ref, wt_ref, qt_ref, o_ref,
                       ikey_ref, thr_ref, m_ref, l_ref, acc_ref, s_ref, p_ref, alpha_ref):
    qb = pl.program_id(1)
    n_chunks = ((qb + 1) * tq + tk - 1) // tk
    qpos = qb * tq + lax.broadcasted_iota(I32, (tk, tq), 1)
    row = lax.broadcasted_iota(I32, (tk, tq), 0)
    w = wt_ref[...] * (IDX_QW ** -0.5)

    def score_chunk(c, carry):
        rows = pl.ds(pl.multiple_of(c * tk, tk), tk)
        dots = jnp.dot(kidx_ref[rows, :], qit_ref[...], preferred_element_type=F32)
        sc = jnp.zeros((tk, tq), F32)
        for h in range(IDX_H):
            sc = sc + jnp.maximum(dots[:, h * tq:(h + 1) * tq], 0.0) * w[h:h + 1, :]
        sc = jnp.where(c * tk + row <= qpos, sc, -jnp.inf)
        ikey_ref[rows, :] = _sortable_key(sc)
        return carry

    lax.fori_loop(0, n_chunks, score_chunk, 0)

    part = 32

    def fold(x, op):
        return op(x.reshape(tk // part, part, tq), axis=0)

    def count(pred):
        def body(c, acc):
            rows = pl.ds(pl.multiple_of(c * tk, tk), tk)
            hit = pred(ikey_ref[rows, :], c * tk + row)
            return acc + fold(jnp.where(hit, 1.0, 0.0), jnp.sum)
        acc = lax.fori_loop(0, n_chunks, body, jnp.zeros((part, tq), F32))
        return jnp.sum(acc, axis=0, keepdims=True)

    thr = _kth_largest_key(lambda cand: count(lambda key, pos: key >= cand), k_sel, (1, tq))
    def count_gt_eq(c, acc):
        rows = pl.ds(pl.multiple_of(c * tk, tk), tk)
        key = ikey_ref[rows, :]
        return (acc[0] + fold(jnp.where(key > thr, 1.0, 0.0), jnp.sum),
                acc[1] + fold(jnp.where(key == thr, 1.0, 0.0), jnp.sum))

    zeros = jnp.zeros((part, tq), F32)
    n_gt, n_eq = (jnp.sum(a, axis=0, keepdims=True)
                  for a in lax.fori_loop(0, n_chunks, count_gt_eq, (zeros, zeros)))
    need = k_sel - n_gt
    thr_ref[0:1, :] = thr
    thr_ref[1:2, :] = jnp.full((1, tq), 2 ** 30, I32)

    @pl.when(jnp.max(n_eq - need) > 0.0)
    def _():
        thr_ref[1:2, :] = _tie_index_limit(
            lambda cand: count(lambda key, pos: (key == thr) & (pos < cand)), need, idx_bits, (1, tq))

    m_ref[...] = jnp.full(m_ref.shape, -jnp.inf, F32)
    l_ref[...] = jnp.zeros(l_ref.shape, F32)
    acc_ref[...] = jnp.zeros(acc_ref.shape, F32)
    tie_j = thr_ref[1:2, :]

    def attend_chunk(c, carry):
        rows = pl.ds(pl.multiple_of(c * tk, tk), tk)
        key = ikey_ref[rows, :]
        pos = c * tk + row
        sel = ((key > thr) | ((key == thr) & (pos <= tie_j))) & (pos <= qpos)
        bias = jnp.where(sel, 0.0, -jnp.inf)
        for j in range(ATTN_H // 2):
            s_ref[:, 2 * j * tq:2 * (j + 1) * tq] = jnp.dot(
                k_ref[rows, j * LANES:(j + 1) * LANES], qt_ref[j], preferred_element_type=F32)
        for h in range(ATTN_H):
            cols = slice(h * tq, (h + 1) * tq)
            s = s_ref[:, cols] + bias
            m_old = m_ref[h]
            m_new = jnp.maximum(m_old, jnp.max(fold(s, jnp.max), axis=0, keepdims=True))
            m_safe = jnp.where(m_new == -jnp.inf, 0.0, m_new)
            alpha = jnp.exp2(m_old - m_safe)
            p = jnp.exp2(s - m_safe[0:1, :])
            p_ref[:, cols] = p.astype(BF16)
            l_ref[h] = alpha * l_ref[h] + jnp.sum(fold(p, jnp.sum), axis=0, keepdims=True)
            alpha_ref[h] = alpha
            m_ref[h] = m_new
        for h in range(ATTN_H):
            hs = slice(h * HEAD_DIM, (h + 1) * HEAD_DIM)
            acc_ref[hs, :] = alpha_ref[h][0:1, :] * acc_ref[hs, :] + jnp.dot(
                vt_ref[c, hs, :], p_ref[:, h * tq:(h + 1) * tq], preferred_element_type=F32)
        return carry

    lax.fori_loop(0, n_chunks, attend_chunk, 0)
    for h in range(ATTN_H):
        hs = slice(h * HEAD_DIM, (h + 1) * HEAD_DIM)
        acc_ref[hs, :] = acc_ref[hs, :] / l_ref[h][0:1, :]
    o_ref[...] = acc_ref[...].T


def dsa_prompt(q, k, v, qidx, kidx, widx, tq, tk):
    b, t, _ = q.shape
    nqb, nck = t // tq, t // tk
    k_sel = min(TOPK_MAX, t // 4)
    eye2 = jnp.eye(2, dtype=F32) * (HEAD_DIM ** -0.5 * LOG2_E)
    q_t = q.reshape(b, nqb, tq, ATTN_H // 2, 2, HEAD_DIM).transpose(0, 1, 3, 4, 5, 2)
    q_t = (q_t[:, :, :, :, None] * eye2[None, None, None, :, :, None, None])
    q_t = q_t.transpose(0, 1, 2, 4, 5, 3, 6).reshape(b, nqb, ATTN_H // 2, LANES, 2 * tq).astype(BF16)
    v_t = v.reshape(b, nck, tk, ATTN_W).transpose(0, 1, 3, 2).astype(BF16)
    qi_t = qidx.reshape(b, nqb, tq, IDX_H, IDX_D).transpose(0, 1, 4, 3, 2).reshape(
        b, nqb, IDX_D, IDX_H * tq).astype(BF16)
    w_t = widx.reshape(b, nqb, tq, IDX_H).transpose(0, 1, 3, 2)
    once = dict(pipeline_mode=pl.Buffered(1))
    return pl.pallas_call(
        functools.partial(_dsa_prompt_kernel, tq, tk, k_sel, (t - 1).bit_length()),
        grid=(b, nqb),
        in_specs=[
            pl.BlockSpec((None, t, IDX_D), lambda b, i: (b, 0, 0), **once),
            pl.BlockSpec((None, t, ATTN_W), lambda b, i: (b, 0, 0), **once),
            pl.BlockSpec((None, nck, ATTN_W, tk), lambda b, i: (b, 0, 0, 0), **once),
            pl.BlockSpec((None, None, IDX_D, IDX_H * tq), lambda b, i: (b, i, 0, 0)),
            pl.BlockSpec((None, None, IDX_H, tq), lambda b, i: (b, i, 0, 0)),
            pl.BlockSpec((None, None, ATTN_H // 2, LANES, 2 * tq), lambda b, i: (b, i, 0, 0, 0)),
        ],
        out_specs=pl.BlockSpec((None, tq, ATTN_W), lambda b, i: (b, i, 0)),
        out_shape=jax.ShapeDtypeStruct((b, t, ATTN_W), F32),
        scratch_shapes=[
            pltpu.VMEM((t, tq), I32),
            pltpu.VMEM((8, tq), I32),
            pltpu.VMEM((ATTN_H, 8, tq), F32),
            pltpu.VMEM((ATTN_H, 8, tq), F32),
            pltpu.VMEM((ATTN_W, tq), F32),
            pltpu.VMEM((tk, ATTN_H * tq), F32),
            pltpu.VMEM((tk, ATTN_H * tq), BF16),
            pltpu.VMEM((ATTN_H, 8, tq), F32),
        ],
        compiler_params=_cparams("parallel", "arbitrary"),
        name="dsa_prompt",
    )(kidx.astype(BF16), k.astype(BF16), v_t, qi_t, w_t, q_t)


_NT = (((1,), (1,)), ((), ()))


def _page_specs(n_pages, layer, block):
    def spec(j):
        return pl.BlockSpec((None, None) + block,
                            lambda b, pt: (layer, pt[b * n_pages + j]) + (0,) * len(block))
    return [spec(j) for j in range(n_pages)]


def _sample_scores_kernel(n_pages, pt_ref, *refs):
    pages, (qi_ref, w_ref, o_ref) = refs[:n_pages], refs[n_pages:]
    qi = qi_ref[...].astype(BF16)
    w = w_ref[...] * (IDX_QW ** -0.5)
    for j in range(n_pages):
        dots = jnp.dot(qi, pages[j][...].astype(BF16), preferred_element_type=F32)
        o_ref[:, j * PAGE_SIZE:(j + 1) * PAGE_SIZE] = jnp.sum(jnp.maximum(dots, 0.0) * w, axis=0, keepdims=True)


def sample_scores(layer, page_table_flat, cache_kidx_t, qidx, widx, n_pages):
    nb = qidx.shape[0]
    return pl.pallas_call(
        functools.partial(_sample_scores_kernel, n_pages),
        grid_spec=pltpu.PrefetchScalarGridSpec(
            num_scalar_prefetch=1,
            grid=(nb,),
            in_specs=_page_specs(n_pages, layer, (IDX_D, PAGE_SIZE)) + [
                pl.BlockSpec((None, IDX_H, IDX_D), lambda b, pt: (b, 0, 0)),
                pl.BlockSpec((None, IDX_H, 1), lambda b, pt: (b, 0, 0)),
            ],
            out_specs=pl.BlockSpec((None, 1, n_pages * PAGE_SIZE), lambda b, pt: (b, 0, 0)),
        ),
        out_shape=jax.ShapeDtypeStruct((nb, 1, n_pages * PAGE_SIZE), F32),
        compiler_params=_cparams("parallel"),
        name="sample_scores",
    )(page_table_flat, *([cache_kidx_t] * n_pages), qidx, widx)


def _sample_select_kernel(k_sel, idx_bits, sc_ref, qi_ref, kt_ref, w_ref, bias_ref, bias_self_ref):
    nb, nk = sc_ref.shape
    r = lax.broadcasted_iota(I32, (IDX_QW, LANES), 0) // IDX_D
    c = lax.broadcasted_iota(I32, (IDX_QW, LANES), 1)
    seg = jnp.where(r == c, 1.0, 0.0).astype(BF16)
    prod = qi_ref[...].astype(BF16).astype(F32) * kt_ref[...].astype(BF16).astype(F32)
    dots = _mm_lhs2(prod, seg)
    self_sc = jnp.sum(jnp.maximum(dots, 0.0) * (w_ref[...] * (IDX_QW ** -0.5)), axis=1, keepdims=True)
    key_self = _sortable_key(self_sc)
    keys = _sortable_key(sc_ref[...])
    pos = lax.broadcasted_iota(I32, (nb, nk), 1)

    def count(pred):
        past = jnp.sum(jnp.where(pred(keys, pos), 1.0, 0.0), axis=1, keepdims=True)
        return past + jnp.where(pred(key_self, nk), 1.0, 0.0)

    thr = _kth_largest_key(lambda cand: count(lambda key, p: key >= cand), k_sel, (nb, 1))
    need = k_sel - count(lambda key, p: key > thr)
    tie_j = _tie_index_limit(
        lambda cand: count(lambda key, p: (key == thr) & (p < cand)), need, idx_bits, (nb, 1))
    chosen = lambda key, p: (key > thr) | ((key == thr) & (p <= tie_j))
    bias_ref[...] = jnp.where(chosen(keys, pos), 0.0, -jnp.inf)
    bias_self_ref[...] = jnp.broadcast_to(jnp.where(chosen(key_self, nk), 0.0, -jnp.inf), (nb, LANES))


def sample_select(scores, qidx, kidx_tiled, w_pad):
    nb, nk = scores.shape
    k_sel = min(TOPK_MAX, (nk + 1) // 4)
    return pl.pallas_call(
        functools.partial(_sample_select_kernel, k_sel, nk.bit_length()),
        out_shape=[jax.ShapeDtypeStruct((nb, nk), F32), jax.ShapeDtypeStruct((nb, LANES), F32)],
        compiler_params=pltpu.CompilerParams(vmem_limit_bytes=VMEM_LIMIT),
        name="sample_select",
    )(scores, qidx, kidx_tiled, w_pad)


def _sample_attend_kernel(n_pages, pt_ref, *refs):
    k_pages, v_pages = refs[:n_pages], refs[n_pages:2 * n_pages]
    bias_ref, q_ref, kn_ref, vn_ref, bself_ref, o_ref, s_ref = refs[2 * n_pages:]
    own = (lax.broadcasted_iota(I32, (ATTN_H, ATTN_W), 1) // HEAD_DIM
           == lax.broadcasted_iota(I32, (ATTN_H, ATTN_W), 0))
    q_bd = jnp.where(own, q_ref[...], 0.0) * (HEAD_DIM ** -0.5)
    q_bf = q_bd.astype(BF16)
    for j in range(n_pages):
        cols = slice(j * PAGE_SIZE, (j + 1) * PAGE_SIZE)
        k_t = k_pages[j][...].reshape(ATTN_W, PAGE_SIZE).astype(BF16)
        s_ref[:, cols] = jnp.dot(q_bf, k_t, preferred_element_type=F32) + bias_ref[:, cols]
    s = s_ref[...]
    s_self = jnp.sum(q_bd * kn_ref[...], axis=1, keepdims=True) + bself_ref[:, 0:1]
    m = jnp.maximum(jnp.max(s, axis=1, keepdims=True), s_self)
    p = jnp.exp(s - m)
    p_self = jnp.exp(s_self - m)
    denom = jnp.sum(p, axis=1, keepdims=True) + p_self
    acc = p_self * vn_ref[...]
    for j in range(n_pages):
        v_t = v_pages[j][...].reshape(ATTN_W, PAGE_SIZE).astype(BF16)
        acc = acc + lax.dot_general(p[:, j * PAGE_SIZE:(j + 1) * PAGE_SIZE].astype(BF16), v_t, _NT,
                                    preferred_element_type=F32)
    o_ref[...] = jnp.sum(jnp.where(own, acc / denom, 0.0), axis=0, keepdims=True)


def sample_attend(layer, page_table_flat, cache_k_t, cache_v_t, bias, bias_self, q, k_new, v_new, n_pages):
    nb = q.shape[0]
    past = n_pages * PAGE_SIZE
    row = pl.BlockSpec((None, 1, ATTN_W), lambda b, pt: (b, 0, 0))
    pages = _page_specs(n_pages, layer, (ATTN_H, HEAD_DIM, PAGE_SIZE))
    return pl.pallas_call(
        functools.partial(_sample_attend_kernel, n_pages),
        grid_spec=pltpu.PrefetchScalarGridSpec(
            num_scalar_prefetch=1,
            grid=(nb,),
            in_specs=pages + pages + [
                pl.BlockSpec((None, 1, past), lambda b, pt: (b, 0, 0)),
                row, row, row,
                pl.BlockSpec((None, 1, LANES), lambda b, pt: (b, 0, 0))],
            out_specs=row,
            scratch_shapes=[pltpu.VMEM((ATTN_H, past), F32)],
        ),
        out_shape=jax.ShapeDtypeStruct((nb, 1, ATTN_W), F32),
        compiler_params=_cparams("parallel"),
        name="sample_attend",
    )(page_table_flat, *([cache_k_t] * n_pages), *([cache_v_t] * n_pages), bias, q, k_new, v_new, bias_self)


def _ffn_kernel(routed, x_ref, nw_ref, mod_ref, router_ref, wg_ref, wu_ref, wd_ref, y_ref,
                h_ref, acc_ref, comb_ref):
    d = D_MODEL
    j = pl.program_id(2)

    @pl.when(j == 0)
    def _():
        h = _modulated_norm(x_ref[...], nw_ref[...], mod_ref[:, 3 * d:4 * d], mod_ref[:, 4 * d:5 * d])
        h_ref[...] = h.astype(BF16)
        acc_ref[...] = jnp.zeros(acc_ref.shape, F32)
        if routed:
            lane = lax.broadcasted_iota(I32, comb_ref.shape, 1).astype(F32)
            logits = jnp.where(lane < N_EXPERTS, _mm3(h, router_ref[...]), -jnp.inf)
            m1 = jnp.max(logits, axis=1, keepdims=True)
            i1 = jnp.min(jnp.where(logits == m1, lane, float(LANES)), axis=1, keepdims=True)
            rest = jnp.where(lane == i1, -jnp.inf, logits)
            m2 = jnp.max(rest, axis=1, keepdims=True)
            i2 = jnp.min(jnp.where(rest == m2, lane, float(LANES)), axis=1, keepdims=True)
            e2 = jnp.exp(m2 - m1)
            comb_ref[...] = (jnp.where(lane == i1, 1.0 / (1.0 + e2), 0.0)
                             + jnp.where(lane == i2, e2 / (1.0 + e2), 0.0))

    h = h_ref[...]
    t = _silu(jnp.dot(h, wg_ref[...], preferred_element_type=F32)) * jnp.dot(
        h, wu_ref[...], preferred_element_type=F32)
    y = jnp.dot(t.astype(BF16), wd_ref[...], preferred_element_type=F32)
    if routed:
        lane = lax.broadcasted_iota(I32, comb_ref.shape, 1)
        y = y * jnp.sum(jnp.where(lane == j, comb_ref[...], 0.0), axis=1, keepdims=True)
    acc_ref[...] += y

    @pl.when(j == pl.num_programs(2) - 1)
    def _():
        y_ref[...] = x_ref[...] + mod_ref[:, 5 * d:6 * d] * acc_ref[...]


def ffn(x, norm_w, mod, router_pad, wg, wu, wd, tm, routed):
    g, r, d = x.shape
    nj, _, f = wg.shape
    tok = pl.BlockSpec((None, tm, d), lambda g, i, j: (g, i, 0))
    if mod.shape[1] == 1:
        mod_spec = pl.BlockSpec((None, 1, mod.shape[2]), lambda g, i, j: (g, 0, 0))
    else:
        mod_spec = pl.BlockSpec((None, tm, mod.shape[2]), lambda g, i, j: (g, i, 0))
    return pl.pallas_call(
        functools.partial(_ffn_kernel, routed),
        grid=(g, r // tm, nj),
        in_specs=[tok, pl.BlockSpec((1, d), lambda g, i, j: (0, 0)), mod_spec,
                  pl.BlockSpec(router_pad.shape, lambda g, i, j: (0, 0)),
                  pl.BlockSpec((None, d, f), lambda g, i, j: (j, 0, 0)),
                  pl.BlockSpec((None, d, f), lambda g, i, j: (j, 0, 0)),
                  pl.BlockSpec((None, f, d), lambda g, i, j: (j, 0, 0))],
        out_specs=tok,
        out_shape=jax.ShapeDtypeStruct(x.shape, F32),
        scratch_shapes=[pltpu.VMEM((tm, d), BF16), pltpu.VMEM((tm, d), F32), pltpu.VMEM((tm, LANES), F32)],
        compiler_params=_cparams("parallel", "parallel", "arbitrary"),
        name="ffn_routed" if routed else "ffn_dense",
    )(x, norm_w.reshape(1, d), mod, router_pad, wg, wu, wd)


def _layer_params(l, p):
    z = jnp.zeros((LORA_W, RWKV_W), F32)
    w_in = jnp.pad(p["w_in"][l], ((0, 0), (0, SHIFT_W + ATTN_IN_W + IDX_PAD_W - IN_W))).astype(BF16)
    j = l // 2
    if l % 2 == 0:
        halves = lambda w: jnp.stack(jnp.split(w, w.shape[1] // D_FF_TILE, axis=1)).astype(BF16)
        wg, wu = halves(p["ffn_w_gate"][j]), halves(p["ffn_w_up"][j])
        wd = jnp.stack(jnp.split(p["ffn_w_down"][j], wg.shape[0], axis=0)).astype(BF16)
        router = jnp.zeros((8, LANES), F32)
    else:
        wg, wu, wd = (p[n][j].astype(BF16) for n in ("exp_w_gate", "exp_w_up", "exp_w_down"))
        router = jnp.pad(p["router_w"][j], ((0, 0), (0, LANES - N_EXPERTS)))
    return dict(
        norm1_w=p["norm1_w"][l], norm2_w=p["norm2_w"][l], w_in=w_in, mu=p["mu_shift"][l][None, :],
        vecs=jnp.stack([p["rwkv_w0"][l], p["rwkv_a0"][l], p["rwkv_k_k"][l], p["rwkv_k_a"][l],
                        p["rwkv_r_k"][l].reshape(RWKV_W)] + [jnp.zeros((RWKV_W,), F32)] * 3),
        w2=z.at[0:W_LORA].set(p["rwkv_w2"][l]),
        a2=z.at[W_LORA:W_LORA + A_LORA].set(p["rwkv_a2"][l]),
        g2=z.at[W_LORA + A_LORA:].set(p["rwkv_g2"][l]),
        lnx=jnp.stack([p["lnx_w"][l], p["lnx_b"][l]]),
        qkw=jnp.stack([jnp.tile(p["q_norm_w"][l], ATTN_H), jnp.tile(p["k_norm_w"][l], ATTN_H)]),
        w_out=p["w_out"][l].astype(BF16), router=router, wg=wg, wu=wu, wd=wd, routed=(l % 2 == 1))


def _trunk_layer(x, mod, lp, shift_prev, s0, tabs, seq_mode, tm, attend):
    g, r, _ = x.shape
    pr, pa, pi = proj_in(x, lp["norm1_w"], mod, lp["w_in"], tm)
    dr, dd, k2, vr, kk, bb_, gate, bonus, kr = rwkv_prep(pr, shift_prev, lp["mu"], lp["vecs"], lp["w2"],
                                                         lp["a2"], lp["g2"], tm, seq_mode)
    row_operands = (dr, dd, k2, kk, bb_, kr)
    if seq_mode:
        v_t = vr.reshape(g, r, RWKV_H, HEAD_DIM).transpose(0, 2, 3, 1)
        o_tiles, s_fin = rwkv_scan(row_operands, v_t, _pack_state(s0), g, LANES)
        o_scan = _readout_rows(o_tiles, r)
    else:
        o_tiles, s_fin = rwkv_scan([a.reshape(r, 1, RWKV_W) for a in row_operands],
                                   vr.reshape(r, RWKV_H, HEAD_DIM, 1), _pack_state(s0), 8, 1)
        o_scan = _readout_rows(o_tiles, 1).reshape(g, r, RWKV_W)
    q, k, qi, ki = attn_prep(pa, pi, lp["qkw"], tabs[0], tabs[1], tm)
    v = pa[..., 2 * ATTN_W:]
    kidx = ki[..., :IDX_D]
    widx = ki[..., IDX_D:IDX_D + IDX_H]
    o_attn = attend(q, k, v, qi, kidx, widx)
    x = proj_out(x, o_scan, bonus, gate, o_attn, lp["lnx"], mod, lp["w_out"], tm)
    x = ffn(x, lp["norm2_w"], mod, lp["router"], lp["wg"], lp["wu"], lp["wd"], min(512, r), lp["routed"])
    return x, (k, v, kidx, _unpack_state(s_fin), pr[:, -1] if seq_mode else pr[0])


def kernel(x_prompt, x_sample, c_prompt, c_sample, cache_k, cache_v, cache_kidx, state_wkv, state_shift, page_table, norm1_w, norm2_w, w_ada, b_ada, w_in, mu_shift, rwkv_w0, rwkv_w2, rwkv_a0, rwkv_a2, rwkv_g2, rwkv_k_k, rwkv_k_a, rwkv_r_k, lnx_w, lnx_b, q_norm_w, k_norm_w, w_out, ffn_w_gate, ffn_w_up, ffn_w_down, router_w, exp_w_gate, exp_w_up, exp_w_down):
    params = dict(norm1_w=norm1_w, norm2_w=norm2_w, w_in=w_in, mu_shift=mu_shift, rwkv_w0=rwkv_w0,
                  rwkv_w2=rwkv_w2, rwkv_a0=rwkv_a0, rwkv_a2=rwkv_a2, rwkv_g2=rwkv_g2, rwkv_k_k=rwkv_k_k,
                  rwkv_k_a=rwkv_k_a, rwkv_r_k=rwkv_r_k, lnx_w=lnx_w, lnx_b=lnx_b, q_norm_w=q_norm_w,
                  k_norm_w=k_norm_w, w_out=w_out, ffn_w_gate=ffn_w_gate, ffn_w_up=ffn_w_up,
                  ffn_w_down=ffn_w_down, router_w=router_w, exp_w_gate=exp_w_gate, exp_w_up=exp_w_up,
                  exp_w_down=exp_w_down)
    depth = w_in.shape[0]
    bp, t, d = x_prompt.shape
    nb = x_sample.shape[0]
    n_pages = page_table.shape[1]
    past = n_pages * PAGE_SIZE
    pt_flat = page_table.reshape(-1).astype(I32)
    cache_k_t = jnp.transpose(cache_k, (0, 1, 3, 4, 2))
    cache_v_t = jnp.transpose(cache_v, (0, 1, 3, 4, 2))
    cache_kidx_t = jnp.transpose(cache_kidx, (0, 1, 3, 2))

    n_mod = bp + nb
    c_all = jnp.pad(jnp.concatenate([c_prompt, c_sample]), ((0, -n_mod % 8), (0, 0)))
    mod = ada_modulation(c_all, w_ada, b_ada)
    mod_p = mod[:, :bp, None, :]
    mod_s = mod[:, None, bp:n_mod, :]

    def tables(pos):
        return (rope_tables(pos, ATTN_W, HEAD_DIM, ROT_DIM, ATTN_W),
                rope_tables(pos, IDX_PAD_W, IDX_D, IDX_ROT_DIM, IDX_QW + IDX_D))

    tabs_p = tables(jnp.arange(t, dtype=I32))
    tabs_s = tables(jnp.full((1,), past, I32))
    tm_p = min(256, t)
    tq, tk = min(128, t), min(512, t)

    xp, xs = x_prompt, x_sample.reshape(1, nb, d)
    st_p, st_s = [], []
    for l in range(depth):
        lp = _layer_params(l, params)
        attend_p = lambda q, k, v, qi, ki, wi: dsa_prompt(q, k, v, qi, ki, wi, tq, tk)

        def attend_s(q, k, v, qi, ki, wi, l=l):
            rows = lambda a: a.reshape(nb, 1, ATTN_W)
            qi_h = qi.reshape(nb, IDX_H, IDX_D)
            scores = sample_scores(l, pt_flat, cache_kidx_t, qi_h, wi.reshape(nb, IDX_H, 1), n_pages)
            w_pad = jnp.pad(wi.reshape(nb, IDX_H), ((0, 0), (0, LANES - IDX_H)))
            bias, bias_self = sample_select(scores.reshape(nb, past), qi.reshape(nb, IDX_QW),
                                            jnp.tile(ki.reshape(nb, IDX_D), (1, IDX_H)), w_pad)
            o = sample_attend(l, pt_flat, cache_k_t, cache_v_t, bias.reshape(nb, 1, past),
                              bias_self.reshape(nb, 1, LANES), rows(q), rows(k), rows(v), n_pages)
            return o.reshape(1, nb, ATTN_W)

        xp, sp = _trunk_layer(xp, mod_p[l], lp, jnp.zeros((bp, 1, SHIFT_W), F32),
                              jnp.zeros((bp, RWKV_H, HEAD_DIM, HEAD_DIM), F32), tabs_p, True, tm_p, attend_p)
        xs, ss = _trunk_layer(xs, mod_s[l], lp, state_shift[l][None], state_wkv[l], tabs_s, False, nb, attend_s)
        st_p.append(sp)
        st_s.append(ss)

    def stack(states, i, shape):
        return jnp.stack([s[i] for s in states]).reshape(shape)

    return (xp, xs.reshape(nb, 1, d),
            stack(st_p, 0, (depth, bp, t, ATTN_H, HEAD_DIM)), stack(st_p, 1, (depth, bp, t, ATTN_H, HEAD_DIM)),
            stack(st_p, 2, (depth, bp, t, IDX_D)), stack(st_p, 3, (depth, bp, RWKV_H, HEAD_DIM, HEAD_DIM)),
            stack(st_p, 4, (depth, bp, SHIFT_W)),
            stack(st_s, 0, (depth, nb, 1, ATTN_H, HEAD_DIM)), stack(st_s, 1, (depth, nb, 1, ATTN_H, HEAD_DIM)),
            stack(st_s, 2, (depth, nb, 1, IDX_D)), stack(st_s, 3, (depth, nb, RWKV_H, HEAD_DIM, HEAD_DIM)),
            stack(st_s, 4, (depth, nb, SHIFT_W)))
```

```python
import functools

import jax
import jax.numpy as jnp
from jax import lax
from jax.experimental import pallas as pl
from jax.experimental.pallas import tpu as pltpu

F32 = jnp.float32
BF16 = jnp.bfloat16
I32 = jnp.int32

D_MODEL = 1024
HEAD_DIM = 64
RWKV_W = 512
RWKV_H = 8
ATTN_W = 512
ATTN_H = 8
W_LORA = 64
A_LORA = 64
G_LORA = 128
LORA_W = W_LORA + A_LORA + G_LORA
IDX_H = 8
IDX_D = 32
IDX_QW = IDX_H * IDX_D
TOPK_MAX = 256
ROT_DIM = 16
IDX_ROT_DIM = 8
ROPE_THETA = 500000.0
PAGE_SIZE = 128
D_FF_TILE = 1408
N_EXPERTS = 8
NORM_EPS = 1e-6
LNX_EPS = 64e-5
SHIFT_W = 3 * RWKV_W + LORA_W
ATTN_IN_W = ATTN_W + 2 * ATTN_W
IDX_IN_W = IDX_QW + IDX_D + IDX_H
IDX_PAD_W = 384
IN_W = SHIFT_W + ATTN_IN_W + IDX_IN_W

LANES = 128
VMEM_LIMIT = 56 * 1024 * 1024
INT_MIN = -(2 ** 31)
LOG2_E = 1.4426950408889634


def _cparams(*sem):
    return pltpu.CompilerParams(dimension_semantics=sem, vmem_limit_bytes=VMEM_LIMIT)


def _mm(a, b):
    return jnp.dot(a.astype(BF16), b.astype(BF16), preferred_element_type=F32)


def _split2(a):
    hi = a.astype(BF16)
    lo = (a - hi.astype(F32)).astype(BF16)
    return hi, lo


def _mm_lhs2(a, b_bf16):
    hi, lo = _split2(a)
    return (jnp.dot(hi, b_bf16, preferred_element_type=F32)
            + jnp.dot(lo, b_bf16, preferred_element_type=F32))


def _mm3(a, b):
    a_hi, a_lo = _split2(a)
    b_hi, b_lo = _split2(b)
    return (jnp.dot(a_hi, b_hi, preferred_element_type=F32)
            + jnp.dot(a_lo, b_hi, preferred_element_type=F32)
            + jnp.dot(a_hi, b_lo, preferred_element_type=F32))


def _silu(x):
    return x * jax.nn.sigmoid(x)


def _seg_ones(width, seg, scale=1.0):
    r = lax.broadcasted_iota(I32, (width, width), 0) // seg
    c = lax.broadcasted_iota(I32, (width, width), 1) // seg
    return jnp.where(r == c, scale, 0.0).astype(BF16)


def _ada_kernel(c_ref, w_ref, b_ref, o_ref):
    o_ref[...] = _mm3(_silu(c_ref[...]), w_ref[...]) + b_ref[...]


def ada_modulation(c_all, w_ada, b_ada):
    m, d = c_all.shape
    nl, _, n = w_ada.shape
    tn = 1536
    return pl.pallas_call(
        _ada_kernel,
        grid=(nl, n // tn),
        in_specs=[
            pl.BlockSpec((m, d), lambda l, j: (0, 0)),
            pl.BlockSpec((None, d, tn), lambda l, j: (l, 0, j)),
            pl.BlockSpec((None, 1, tn), lambda l, j: (l, 0, j)),
        ],
        out_specs=pl.BlockSpec((None, m, tn), lambda l, j: (l, 0, j)),
        out_shape=jax.ShapeDtypeStruct((nl, m, n), F32),
        compiler_params=_cparams("parallel", "parallel"),
        name="ada_modulation",
    )(c_all, w_ada, b_ada.reshape(nl, 1, n))


def _modulated_norm(x, norm_w, shift, scale):
    y = x * lax.rsqrt(jnp.mean(x * x, axis=-1, keepdims=True) + NORM_EPS)
    return (y * norm_w) * (1.0 + scale) + shift


def _proj_in_kernel(x_ref, nw_ref, mod_ref, w_ref, pr_ref, pa_ref, pi_ref):
    d = D_MODEL
    h = _modulated_norm(x_ref[...], nw_ref[...], mod_ref[:, 0:d], mod_ref[:, d:2 * d])
    p = jnp.dot(h.astype(BF16), w_ref[...], preferred_element_type=F32)
    pr_ref[...] = p[:, :SHIFT_W]
    pa_ref[...] = p[:, SHIFT_W:SHIFT_W + ATTN_IN_W]
    pi_ref[...] = p[:, SHIFT_W + ATTN_IN_W:]


def _mod_spec(mod, tm):
    if mod.shape[1] == 1:
        return pl.BlockSpec((None, 1, mod.shape[2]), lambda g, i: (g, 0, 0))
    return pl.BlockSpec((None, tm, mod.shape[2]), lambda g, i: (g, i, 0))


def proj_in(x, norm_w, mod, w_in_bf16, tm):
    g, r, d = x.shape
    n = w_in_bf16.shape[1]
    widths = (SHIFT_W, ATTN_IN_W, n - SHIFT_W - ATTN_IN_W)
    return pl.pallas_call(
        _proj_in_kernel,
        grid=(g, r // tm),
        in_specs=[
            pl.BlockSpec((None, tm, d), lambda g, i: (g, i, 0)),
            pl.BlockSpec((1, d), lambda g, i: (0, 0)),
            _mod_spec(mod, tm),
            pl.BlockSpec((d, n), lambda g, i: (0, 0)),
        ],
        out_specs=[pl.BlockSpec((None, tm, w), lambda g, i: (g, i, 0)) for w in widths],
        out_shape=[jax.ShapeDtypeStruct((g, r, w), F32) for w in widths],
        compiler_params=_cparams("parallel", "parallel"),
        name="proj_in",
    )(x, norm_w.reshape(1, d), mod, w_in_bf16)


def _rwkv_prep_kernel(seq_mode, p_ref, halo_ref, sp_ref, mu_ref, vec_ref, w2_ref, a2_ref, g2_ref,
                      dr_ref, d_ref, k_ref, v_ref, kk_ref, b_ref, g_ref, bonus_ref, kr_ref):
    p = p_ref[...]
    if seq_mode:
        first = jnp.where(pl.program_id(1) == 0, sp_ref[...], halo_ref[7:8, :])
        row = lax.broadcasted_iota(I32, p.shape, 0)
        prev = jnp.where(row == 0, first, pltpu.roll(p, 1, axis=0))
    else:
        prev = sp_ref[...]
    xs = p + mu_ref[...] * (prev - p)
    w = RWKV_W
    r = xs[:, 0:w]
    k = xs[:, w:2 * w]
    v = xs[:, 2 * w:3 * w]
    lo = xs[:, 3 * w:]
    lane = lax.broadcasted_iota(I32, lo.shape, 1)
    act = jnp.where(lane < W_LORA, jnp.tanh(lo),
                    jnp.where(lane < W_LORA + A_LORA, lo, jax.nn.sigmoid(lo)))
    w0, a0, k_k, k_a, r_k = (vec_ref[i:i + 1, :] for i in range(5))
    z = -(w0 + _mm3(act, w2_ref[...]))
    softplus = jnp.maximum(z, 0.0) + jnp.log(1.0 + jnp.exp(-jnp.abs(z)))
    decay = jnp.exp(-jnp.exp(-softplus - 0.5))
    a = jax.nn.sigmoid(a0 + _mm3(act, a2_ref[...]))
    g = _mm3(act, g2_ref[...])
    seg = _seg_ones(w, HEAD_DIM)
    kk = k * k_k
    kk = kk * lax.rsqrt(jnp.maximum(_mm_lhs2(kk * kk, seg), 1e-24))
    k2 = k * (1.0 + (a - 1.0) * k_a)
    b = kk * a
    dr_ref[...] = decay * r - _mm_lhs2(b * r, seg) * kk
    d_ref[...] = decay
    k_ref[...] = k2
    v_ref[...] = v
    kk_ref[...] = kk
    b_ref[...] = b
    g_ref[...] = g
    bonus_ref[...] = _mm_lhs2(r * k2 * r_k, seg) * v
    kr_ref[...] = _mm_lhs2(k2 * r, seg)


def rwkv_prep(p, shift_prev, mu, vecs, w2p, a2p, g2p, tm, seq_mode):
    g, r, sw = p.shape
    w = RWKV_W
    tok = lambda width: pl.BlockSpec((None, tm, width), lambda g, i: (g, i, 0))
    full = lambda a: pl.BlockSpec(a.shape, lambda g, i: (0,) * a.ndim)
    if seq_mode:
        halo_spec = pl.BlockSpec((None, 8, sw), lambda g, i: (g, jnp.maximum(i * (tm // 8) - 1, 0), 0))
        sp_spec = pl.BlockSpec((None, 1, sw), lambda g, i: (g, 0, 0))
    else:
        halo_spec = pl.BlockSpec((None, 8, sw), lambda g, i: (g, 0, 0))
        sp_spec = tok(sw)
    return pl.pallas_call(
        functools.partial(_rwkv_prep_kernel, seq_mode),
        grid=(g, r // tm),
        in_specs=[tok(sw), halo_spec, sp_spec, full(mu), full(vecs), full(w2p), full(a2p), full(g2p)],
        out_specs=[tok(w)] * 9,
        out_shape=[jax.ShapeDtypeStruct((g, r, w), F32)] * 9,
        compiler_params=_cparams("parallel", "arbitrary"),
        name="rwkv_prep",
    )(p, p, shift_prev, mu, vecs, w2p, a2p, g2p)


OUT_WINDOW = 64


def _rwkv_scan_kernel(bb_n, tt, dr_ref, d_ref, k_ref, kk_ref, b_ref, kr_ref, vt_ref, s0_ref,
                      ot_ref, st_ref):
    @pl.when(pl.program_id(1) == 0)
    def _():
        st_ref[...] = s0_ref[...]

    ot_ref[...] = jnp.zeros_like(ot_ref)
    lane = lax.broadcasted_iota(I32, (HEAD_DIM, LANES), 1)
    lane_a = lane < HEAD_DIM
    t_lane = lax.broadcasted_iota(I32, (HEAD_DIM, tt), 1)
    pairs = [(bb, j) for bb in range(bb_n) for j in range(RWKV_H // 2)]
    ones_r = lax.broadcasted_iota(I32, (2 * LANES, LANES), 0) % LANES // HEAD_DIM
    ones_c = lax.broadcasted_iota(I32, (2 * LANES, LANES), 1) // HEAD_DIM
    stacked_ones = jnp.where(ones_r == ones_c, 1.0, 0.0).astype(BF16)

    def head_sums(x):
        hi, lo = _split2(x)
        return jnp.dot(jnp.concatenate([hi, lo], axis=1), stacked_ones, preferred_element_type=F32)

    def step(t, rows_of, u):
        hot = t_lane == t
        hot_out = (lane % OUT_WINDOW) == (t % OUT_WINDOW)
        window = t // OUT_WINDOW

        def rows_all(ref):
            return jnp.concatenate(
                [jnp.broadcast_to(rows_of(ref, bb, slice(j * LANES, (j + 1) * LANES))[u:u + 1, :],
                                  (HEAD_DIM, LANES)) for bb, j in pairs], axis=0)

        def value_cols(bb, j):
            v_a = jnp.sum(jnp.where(hot, vt_ref[bb, 2 * j], 0.0), axis=1, keepdims=True)
            v_b = jnp.sum(jnp.where(hot, vt_ref[bb, 2 * j + 1], 0.0), axis=1, keepdims=True)
            return jnp.where(lane_a, v_a, v_b)

        s = jnp.concatenate([st_ref[bb, j] for bb, j in pairs], axis=0)
        sk = head_sums(s * rows_all(kk_ref))
        s_dr = jnp.dot((s * rows_all(dr_ref)).astype(BF16), stacked_ones[:LANES], preferred_element_type=F32)
        vv = jnp.concatenate([value_cols(bb, j) for bb, j in pairs], axis=0)
        s_new = s * rows_all(d_ref) - sk * rows_all(b_ref) + vv * rows_all(k_ref)
        out = s_dr + rows_all(kr_ref) * vv
        for i, (bb, j) in enumerate(pairs):
            rows = slice(i * HEAD_DIM, (i + 1) * HEAD_DIM)
            st_ref[bb, j] = s_new[rows]
            ot_ref[bb, j, window] = jnp.where(hot_out, out[rows], ot_ref[bb, j, window])

    sub = 8
    if tt % sub:
        for t in range(tt):
            step(t, lambda ref, bb, ls: ref[bb, :, ls], t)
    else:
        def group(g, carry):
            base = pl.multiple_of(g * sub, sub)
            for u in range(sub):
                step(base + u, lambda ref, bb, ls: ref[bb, pl.ds(base, sub), ls], u)
            return carry

        lax.fori_loop(0, tt // sub, group, 0)


def rwkv_scan(row_operands, v_t, s0_packed, bb, tt):
    ns, t, w = row_operands[0].shape
    n_win, win_blk = pl.cdiv(t, OUT_WINDOW), pl.cdiv(tt, OUT_WINDOW)
    row_spec = pl.BlockSpec((bb, tt, w), lambda n, i: (n, i, 0))
    vt_spec = pl.BlockSpec((bb, RWKV_H, HEAD_DIM, tt), lambda n, i: (n, 0, 0, i))
    st_spec = pl.BlockSpec((bb, RWKV_H // 2, HEAD_DIM, LANES), lambda n, i: (n, 0, 0, 0))
    ot_spec = pl.BlockSpec((bb, RWKV_H // 2, win_blk, HEAD_DIM, LANES), lambda n, i: (n, 0, i, 0, 0))
    return pl.pallas_call(
        functools.partial(_rwkv_scan_kernel, bb, tt),
        grid=(ns // bb, t // tt),
        in_specs=[row_spec] * len(row_operands) + [vt_spec, st_spec],
        out_specs=[ot_spec, st_spec],
        out_shape=[jax.ShapeDtypeStruct((ns, RWKV_H // 2, n_win, HEAD_DIM, LANES), F32),
                   jax.ShapeDtypeStruct(s0_packed.shape, F32)],
        compiler_params=_cparams("parallel", "arbitrary"),
        name="rwkv_scan",
    )(*row_operands, v_t, s0_packed)


def _readout_rows(o_tiles, t):
    ns, _, n_win = o_tiles.shape[:3]
    o = o_tiles.reshape(ns, RWKV_H // 2, n_win, HEAD_DIM, 2, OUT_WINDOW).transpose(0, 2, 5, 1, 4, 3)
    return o.reshape(ns, n_win * OUT_WINDOW, RWKV_W)[:, :t]


def _pack_state(s):
    ns = s.shape[0]
    return (s.reshape(ns, RWKV_H // 2, 2, HEAD_DIM, HEAD_DIM).transpose(0, 1, 3, 2, 4)
            .reshape(ns, RWKV_H // 2, HEAD_DIM, LANES))


def _unpack_state(s):
    ns = s.shape[0]
    return (s.reshape(ns, RWKV_H // 2, HEAD_DIM, 2, HEAD_DIM).transpose(0, 1, 3, 2, 4)
            .reshape(ns, RWKV_H, HEAD_DIM, HEAD_DIM))


def _rwkv_readout(o, bonus, g, lnx_w, lnx_b):
    avg = _seg_ones(RWKV_W, HEAD_DIM, 1.0 / HEAD_DIM)
    mean = _mm_lhs2(o, avg)
    c = o - mean
    var = _mm_lhs2(c * c, avg)
    return ((c * lax.rsqrt(var + LNX_EPS)) * lnx_w + lnx_b + bonus) * g


def _proj_out_kernel(x_ref, o_ref, bonus_ref, g_ref, oa_ref, ln_ref, mod_ref, w_ref, y_ref):
    d = D_MODEL
    o_rwkv = _rwkv_readout(o_ref[...], bonus_ref[...], g_ref[...], ln_ref[0:1, :], ln_ref[1:2, :])
    mix = jnp.concatenate([o_rwkv, oa_ref[...]], axis=-1).astype(BF16)
    y = jnp.dot(mix, w_ref[...], preferred_element_type=F32)
    y_ref[...] = x_ref[...] + mod_ref[:, 2 * d:3 * d] * y


def proj_out(x, o_scan, bonus, g, o_attn, lnx, mod, w_out_bf16, tm):
    gg, r, d = x.shape
    tok = lambda width: pl.BlockSpec((None, tm, width), lambda g, i: (g, i, 0))
    full = lambda a: pl.BlockSpec(a.shape, lambda g, i: (0,) * a.ndim)
    return pl.pallas_call(
        _proj_out_kernel,
        grid=(gg, r // tm),
        in_specs=[tok(d), tok(RWKV_W), tok(RWKV_W), tok(RWKV_W), tok(ATTN_W), full(lnx),
                  _mod_spec(mod, tm), full(w_out_bf16)],
        out_specs=tok(d),
        out_shape=jax.ShapeDtypeStruct(x.shape, F32),
        compiler_params=_cparams("parallel", "parallel"),
        name="proj_out",
    )(x, o_scan, bonus, g, o_attn, lnx, mod, w_out_bf16)


def rope_tables(pos, width, group, rot_dim, rot_width):
    half = rot_dim // 2
    inv_freq = ROPE_THETA ** (-jnp.arange(half, dtype=F32) / half)
    ang = pos.astype(F32)[:, None] * inv_freq[None, :]
    lane = jnp.arange(width)
    j = lane % group
    rot = (lane < rot_width) & (j < rot_dim)
    cos = jnp.where(rot[None, :], jnp.cos(ang)[:, j % half], 1.0)
    sin = jnp.sin(ang)[:, j % half]
    sin_lo = jnp.where((rot & (j < half))[None, :], -sin, 0.0)
    sin_hi = jnp.where((rot & (j >= half))[None, :], sin, 0.0)
    return jnp.stack([cos, sin_lo, sin_hi]).astype(F32)


def _rope(x, tab_ref, half):
    width = x.shape[-1]
    return (x * tab_ref[0] + pltpu.roll(x, width - half, axis=1) * tab_ref[1]
            + pltpu.roll(x, half, axis=1) * tab_ref[2])


def _attn_prep_kernel(pa_ref, pi_ref, qkw_ref, ta_ref, ti_ref, q_ref, k_ref, qi_ref, ki_ref):
    avg = _seg_ones(ATTN_W, HEAD_DIM, 1.0 / HEAD_DIM)

    def head_norm(x, w):
        return x * lax.rsqrt(_mm_lhs2(x * x, avg) + NORM_EPS) * w

    q = head_norm(pa_ref[:, 0:ATTN_W], qkw_ref[0:1, :])
    k = head_norm(pa_ref[:, ATTN_W:2 * ATTN_W], qkw_ref[1:2, :])
    q_ref[...] = _rope(q, ta_ref, ROT_DIM // 2)
    k_ref[...] = _rope(k, ta_ref, ROT_DIM // 2)
    pi = _rope(pi_ref[...], ti_ref, IDX_ROT_DIM // 2)
    qi_ref[...] = pi[:, 0:IDX_QW]
    ki_ref[...] = pi[:, IDX_QW:]


def attn_prep(pa, pi, qk_norm_w, tab_attn, tab_idx, tm):
    g, r, _ = pa.shape
    tok = lambda width: pl.BlockSpec((None, tm, width), lambda g, i: (g, i, 0))

    def tab_spec(t):
        if t.shape[1] == 1:
            return pl.BlockSpec((3, 1, t.shape[2]), lambda g, i: (0, 0, 0))
        return pl.BlockSpec((3, tm, t.shape[2]), lambda g, i: (0, i, 0))

    widths = (ATTN_W, ATTN_W, IDX_QW, IDX_PAD_W - IDX_QW)
    return pl.pallas_call(
        _attn_prep_kernel,
        grid=(g, r // tm),
        in_specs=[tok(ATTN_IN_W), tok(IDX_PAD_W), pl.BlockSpec(qk_norm_w.shape, lambda g, i: (0, 0)),
                  tab_spec(tab_attn), tab_spec(tab_idx)],
        out_specs=[tok(w) for w in widths],
        out_shape=[jax.ShapeDtypeStruct((g, r, w), F32) for w in widths],
        compiler_params=_cparams("parallel", "parallel"),
        name="attn_prep",
    )(pa, pi, qk_norm_w, tab_attn, tab_idx)


def _sortable_key(score):
    bits = pltpu.bitcast(score + 0.0, I32)
    return bits ^ ((bits >> 31) & 0x7FFFFFFF)


def _kth_largest_key(count_ge, k_sel, shape):
    t = jnp.where(count_ge(jnp.zeros(shape, I32)) >= k_sel, 0, INT_MIN).astype(I32)

    def body(i, t):
        cand = t | jnp.left_shift(jnp.int32(1), 30 - i)
        return jnp.where(count_ge(cand) >= k_sel, cand, t)

    return lax.fori_loop(0, 31, body, t)


def _tie_index_limit(count_eq_below, need, n_bits, shape):
    def body(i, j):
        cand = j | jnp.left_shift(jnp.int32(1), n_bits - 1 - i)
        return jnp.where(count_eq_below(cand) < need, cand, j)

    return lax.fori_loop(0, n_bits, body, jnp.zeros(shape, I32))


def _dsa_prompt_kernel(tq, tk, k_sel, idx_bits, kidx_ref, k_ref, vt_ref, qit_ref, wt_ref, qt_ref, o_ref,
                       ikey_ref, thr_ref, m_ref, l_ref, acc_ref, s_ref, p_ref, alpha_ref):
    qb = pl.program_id(1)
    n_chunks = ((qb + 1) * tq + tk - 1) // tk
    qpos = qb * tq + lax.broadcasted_iota(I32, (tk, tq), 1)
    row = lax.broadcasted_iota(I32, (tk, tq), 0)
    w = wt_ref[...] * (IDX_QW ** -0.5)

    def score_chunk(c, carry):
        rows = pl.ds(pl.multiple_of(c * tk, tk), tk)
        dots = jnp.dot(kidx_ref[rows, :], qit_ref[...], preferred_element_type=F32)
        sc = jnp.zeros((tk, tq), F32)
        for h in range(IDX_H):
            sc = sc + jnp.maximum(dots[:, h * tq:(h + 1) * tq], 0.0) * w[h:h + 1, :]
        sc = jnp.where(c * tk + row <= qpos, sc, -jnp.inf)
        ikey_ref[rows, :] = _sortable_key(sc)
        return carry

    lax.fori_loop(0, n_chunks, score_chunk, 0)

    part = 32

    def fold(x, op):
        return op(x.reshape(tk // part, part, tq), axis=0)

    def count(pred):
        def body(c, acc):
            rows = pl.ds(pl.multiple_of(c * tk, tk), tk)
            hit = pred(ikey_ref[rows, :], c * tk + row)
            return acc + fold(jnp.where(hit, 1.0, 0.0), jnp.sum)
        acc = lax.fori_loop(0, n_chunks, body, jnp.zeros((part, tq), F32))
        return jnp.sum(acc, axis=0, keepdims=True)

    thr = _kth_largest_key(lambda cand: count(lambda key, pos: key >= cand), k_sel, (1, tq))
    def count_gt_eq(c, acc):
        rows = pl.ds(pl.multiple_of(c * tk, tk), tk)
        key = ikey_ref[rows, :]
        return (acc[0] + fold(jnp.where(key > thr, 1.0, 0.0), jnp.sum),
                acc[1] + fold(jnp.where(key == thr, 1.0, 0.0), jnp.sum))

    zeros = jnp.zeros((part, tq), F32)
    n_gt, n_eq = (jnp.sum(a, axis=0, keepdims=True)
                  for a in lax.fori_loop(0, n_chunks, count_gt_eq, (zeros, zeros)))
    need = k_sel - n_gt
    thr_ref[0:1, :] = thr
    thr_ref[1:2, :] = jnp.full((1, tq), 2 ** 30, I32)

    @pl.when(jnp.max(n_eq - need) > 0.0)
    def _():
        thr_ref[1:2, :] = _tie_index_limit(
            lambda cand: count(lambda key, pos: (key == thr) & (pos < cand)), need, idx_bits, (1, tq))

    m_ref[...] = jnp.full(m_ref.shape, -jnp.inf, F32)
    l_ref[...] = jnp.zeros(l_ref.shape, F32)
    acc_ref[...] = jnp.zeros(acc_ref.shape, F32)
    tie_j = thr_ref[1:2, :]

    def attend_two_chunks(i, carry):
        chunks = (2 * i, jnp.minimum(2 * i + 1, n_chunks - 1))
        live = (True, 2 * i + 1 < n_chunks)
        biases = []
        for slot, c in enumerate(chunks):
            rows = pl.ds(pl.multiple_of(c * tk, tk), tk)
            key = ikey_ref[rows, :]
            pos = c * tk + row
            sel = ((key > thr) | ((key == thr) & (pos <= tie_j))) & (pos <= qpos) & live[slot]
            biases.append(jnp.where(sel, 0.0, -jnp.inf))
            for j in range(ATTN_H // 2):
                s_ref[slot, :, 2 * j * tq:2 * (j + 1) * tq] = jnp.dot(
                    k_ref[rows, j * LANES:(j + 1) * LANES], qt_ref[j], preferred_element_type=F32)
        for slot, c in enumerate(chunks):
            for h in range(ATTN_H):
                cols = slice(h * tq, (h + 1) * tq)
                s = s_ref[slot, :, cols] + biases[slot]
                m_old = m_ref[h]
                m_new = jnp.maximum(m_old, jnp.max(fold(s, jnp.max), axis=0, keepdims=True))
                m_safe = jnp.where(m_new == -jnp.inf, 0.0, m_new)
                alpha = jnp.exp2(m_old - m_safe)
                p = jnp.exp2(s - m_safe[0:1, :])
                p_ref[slot, :, cols] = p.astype(BF16)
                l_ref[h] = alpha * l_ref[h] + jnp.sum(fold(p, jnp.sum), axis=0, keepdims=True)
                alpha_ref[h] = alpha
                m_ref[h] = m_new
            for h in range(ATTN_H):
                hs = slice(h * HEAD_DIM, (h + 1) * HEAD_DIM)
                acc_ref[hs, :] = alpha_ref[h][0:1, :] * acc_ref[hs, :] + jnp.dot(
                    vt_ref[c, hs, :], p_ref[slot, :, h * tq:(h + 1) * tq], preferred_element_type=F32)
        return carry

    lax.fori_loop(0, (n_chunks + 1) // 2, attend_two_chunks, 0)
    for h in range(ATTN_H):
        hs = slice(h * HEAD_DIM, (h + 1) * HEAD_DIM)
        acc_ref[hs, :] = acc_ref[hs, :] / l_ref[h][0:1, :]
    o_ref[...] = acc_ref[...].T


def dsa_prompt(q, k, v, qidx, kidx, widx, tq, tk):
    b, t, _ = q.shape
    nqb, nck = t // tq, t // tk
    k_sel = min(TOPK_MAX, t // 4)
    eye2 = jnp.eye(2, dtype=F32) * (HEAD_DIM ** -0.5 * LOG2_E)
    q_t = q.reshape(b, nqb, tq, ATTN_H // 2, 2, HEAD_DIM).transpose(0, 1, 3, 4, 5, 2)
    q_t = (q_t[:, :, :, :, None] * eye2[None, None, None, :, :, None, None])
    q_t = q_t.transpose(0, 1, 2, 4, 5, 3, 6).reshape(b, nqb, ATTN_H // 2, LANES, 2 * tq).astype(BF16)
    v_t = v.reshape(b, nck, tk, ATTN_W).transpose(0, 1, 3, 2).astype(BF16)
    qi_t = qidx.reshape(b, nqb, tq, IDX_H, IDX_D).transpose(0, 1, 4, 3, 2).reshape(
        b, nqb, IDX_D, IDX_H * tq).astype(BF16)
    w_t = widx.reshape(b, nqb, tq, IDX_H).transpose(0, 1, 3, 2)
    once = dict(pipeline_mode=pl.Buffered(1))
    return pl.pallas_call(
        functools.partial(_dsa_prompt_kernel, tq, tk, k_sel, (t - 1).bit_length()),
        grid=(b, nqb),
        in_specs=[
            pl.BlockSpec((None, t, IDX_D), lambda b, i: (b, 0, 0), **once),
            pl.BlockSpec((None, t, ATTN_W), lambda b, i: (b, 0, 0), **once),
            pl.BlockSpec((None, nck, ATTN_W, tk), lambda b, i: (b, 0, 0, 0), **once),
            pl.BlockSpec((None, None, IDX_D, IDX_H * tq), lambda b, i: (b, i, 0, 0)),
            pl.BlockSpec((None, None, IDX_H, tq), lambda b, i: (b, i, 0, 0)),
            pl.BlockSpec((None, None, ATTN_H // 2, LANES, 2 * tq), lambda b, i: (b, i, 0, 0, 0)),
        ],
        out_specs=pl.BlockSpec((None, tq, ATTN_W), lambda b, i: (b, i, 0)),
        out_shape=jax.ShapeDtypeStruct((b, t, ATTN_W), F32),
        scratch_shapes=[
            pltpu.VMEM((t, tq), I32),
            pltpu.VMEM((8, tq), I32),
            pltpu.VMEM((ATTN_H, 8, tq), F32),
            pltpu.VMEM((ATTN_H, 8, tq), F32),
            pltpu.VMEM((ATTN_W, tq), F32),
            pltpu.VMEM((2, tk, ATTN_H * tq), F32),
            pltpu.VMEM((2, tk, ATTN_H * tq), BF16),
            pltpu.VMEM((ATTN_H, 8, tq), F32),
        ],
        compiler_params=_cparams("parallel", "arbitrary"),
        name="dsa_prompt",
    )(kidx.astype(BF16), k.astype(BF16), v_t, qi_t, w_t, q_t)


_NT = (((1,), (1,)), ((), ()))


def _page_specs(n_pages, layer, block):
    def spec(j):
        return pl.BlockSpec((None, None) + block,
                            lambda b, pt: (layer, pt[b * n_pages + j]) + (0,) * len(block))
    return [spec(j) for j in range(n_pages)]


def _sample_scores_kernel(n_pages, pt_ref, *refs):
    pages, (qi_ref, w_ref, o_ref) = refs[:n_pages], refs[n_pages:]
    qi = qi_ref[...].astype(BF16)
    w = w_ref[...] * (IDX_QW ** -0.5)
    for j in range(n_pages):
        dots = jnp.dot(qi, pages[j][...].astype(BF16), preferred_element_type=F32)
        o_ref[:, j * PAGE_SIZE:(j + 1) * PAGE_SIZE] = jnp.sum(jnp.maximum(dots, 0.0) * w, axis=0, keepdims=True)


def sample_scores(layer, page_table_flat, cache_kidx_t, qidx, widx, n_pages):
    nb = qidx.shape[0]
    return pl.pallas_call(
        functools.partial(_sample_scores_kernel, n_pages),
        grid_spec=pltpu.PrefetchScalarGridSpec(
            num_scalar_prefetch=1,
            grid=(nb,),
            in_specs=_page_specs(n_pages, layer, (IDX_D, PAGE_SIZE)) + [
                pl.BlockSpec((None, IDX_H, IDX_D), lambda b, pt: (b, 0, 0)),
                pl.BlockSpec((None, IDX_H, 1), lambda b, pt: (b, 0, 0)),
            ],
            out_specs=pl.BlockSpec((None, 1, n_pages * PAGE_SIZE), lambda b, pt: (b, 0, 0)),
        ),
        out_shape=jax.ShapeDtypeStruct((nb, 1, n_pages * PAGE_SIZE), F32),
        compiler_params=_cparams("parallel"),
        name="sample_scores",
    )(page_table_flat, *([cache_kidx_t] * n_pages), qidx, widx)


def _sample_select_kernel(k_sel, idx_bits, sc_ref, qi_ref, kt_ref, w_ref, bias_ref, bias_self_ref):
    nb, nk = sc_ref.shape
    r = lax.broadcasted_iota(I32, (IDX_QW, LANES), 0) // IDX_D
    c = lax.broadcasted_iota(I32, (IDX_QW, LANES), 1)
    seg = jnp.where(r == c, 1.0, 0.0).astype(BF16)
    prod = qi_ref[...].astype(BF16).astype(F32) * kt_ref[...].astype(BF16).astype(F32)
    dots = _mm_lhs2(prod, seg)
    self_sc = jnp.sum(jnp.maximum(dots, 0.0) * (w_ref[...] * (IDX_QW ** -0.5)), axis=1, keepdims=True)
    key_self = _sortable_key(self_sc)
    keys = _sortable_key(sc_ref[...])
    pos = lax.broadcasted_iota(I32, (nb, nk), 1)

    def count(pred):
        past = jnp.sum(jnp.where(pred(keys, pos), 1.0, 0.0), axis=1, keepdims=True)
        return past + jnp.where(pred(key_self, nk), 1.0, 0.0)

    thr = _kth_largest_key(lambda cand: count(lambda key, p: key >= cand), k_sel, (nb, 1))
    need = k_sel - count(lambda key, p: key > thr)
    tie_j = _tie_index_limit(
        lambda cand: count(lambda key, p: (key == thr) & (p < cand)), need, idx_bits, (nb, 1))
    chosen = lambda key, p: (key > thr) | ((key == thr) & (p <= tie_j))
    bias_ref[...] = jnp.where(chosen(keys, pos), 0.0, -jnp.inf)
    bias_self_ref[...] = jnp.broadcast_to(jnp.where(chosen(key_self, nk), 0.0, -jnp.inf), (nb, LANES))


def sample_select(scores, qidx, kidx_tiled, w_pad):
    nb, nk = scores.shape
    k_sel = min(TOPK_MAX, (nk + 1) // 4)
    return pl.pallas_call(
        functools.partial(_sample_select_kernel, k_sel, nk.bit_length()),
        out_shape=[jax.ShapeDtypeStruct((nb, nk), F32), jax.ShapeDtypeStruct((nb, LANES), F32)],
        compiler_params=pltpu.CompilerParams(vmem_limit_bytes=VMEM_LIMIT),
        name="sample_select",
    )(scores, qidx, kidx_tiled, w_pad)


def _sample_attend_kernel(n_pages, pt_ref, *refs):
    k_pages, v_pages = refs[:n_pages], refs[n_pages:2 * n_pages]
    bias_ref, q_ref, kn_ref, vn_ref, bself_ref, o_ref, s_ref = refs[2 * n_pages:]
    own = (lax.broadcasted_iota(I32, (ATTN_H, ATTN_W), 1) // HEAD_DIM
           == lax.broadcasted_iota(I32, (ATTN_H, ATTN_W), 0))
    q_bd = jnp.where(own, q_ref[...], 0.0) * (HEAD_DIM ** -0.5)
    q_bf = q_bd.astype(BF16)
    for j in range(n_pages):
        cols = slice(j * PAGE_SIZE, (j + 1) * PAGE_SIZE)
        k_t = k_pages[j][...].reshape(ATTN_W, PAGE_SIZE).astype(BF16)
        s_ref[:, cols] = jnp.dot(q_bf, k_t, preferred_element_type=F32) + bias_ref[:, cols]
    s = s_ref[...]
    s_self = jnp.sum(q_bd * kn_ref[...], axis=1, keepdims=True) + bself_ref[:, 0:1]
    m = jnp.maximum(jnp.max(s, axis=1, keepdims=True), s_self)
    p = jnp.exp(s - m)
    p_self = jnp.exp(s_self - m)
    denom = jnp.sum(p, axis=1, keepdims=True) + p_self
    acc = p_self * vn_ref[...]
    for j in range(n_pages):
        v_t = v_pages[j][...].reshape(ATTN_W, PAGE_SIZE).astype(BF16)
        acc = acc + lax.dot_general(p[:, j * PAGE_SIZE:(j + 1) * PAGE_SIZE].astype(BF16), v_t, _NT,
                                    preferred_element_type=F32)
    o_ref[...] = jnp.sum(jnp.where(own, acc / denom, 0.0), axis=0, keepdims=True)


def sample_attend(layer, page_table_flat, cache_k_t, cache_v_t, bias, bias_self, q, k_new, v_new, n_pages):
    nb = q.shape[0]
    past = n_pages * PAGE_SIZE
    row = pl.BlockSpec((None, 1, ATTN_W), lambda b, pt: (b, 0, 0))
    pages = _page_specs(n_pages, layer, (ATTN_H, HEAD_DIM, PAGE_SIZE))
    return pl.pallas_call(
        functools.partial(_sample_attend_kernel, n_pages),
        grid_spec=pltpu.PrefetchScalarGridSpec(
            num_scalar_prefetch=1,
            grid=(nb,),
            in_specs=pages + pages + [
                pl.BlockSpec((None, 1, past), lambda b, pt: (b, 0, 0)),
                row, row, row,
                pl.BlockSpec((None, 1, LANES), lambda b, pt: (b, 0, 0))],
            out_specs=row,
            scratch_shapes=[pltpu.VMEM((ATTN_H, past), F32)],
        ),
        out_shape=jax.ShapeDtypeStruct((nb, 1, ATTN_W), F32),
        compiler_params=_cparams("parallel"),
        name="sample_attend",
    )(page_table_flat, *([cache_k_t] * n_pages), *([cache_v_t] * n_pages), bias, q, k_new, v_new, bias_self)


def _ffn_kernel(routed, x_ref, nw_ref, mod_ref, router_ref, wg_ref, wu_ref, wd_ref, y_ref,
                h_ref, acc_ref, comb_ref):
    d = D_MODEL
    j = pl.program_id(2)

    @pl.when(j == 0)
    def _():
        h = _modulated_norm(x_ref[...], nw_ref[...], mod_ref[:, 3 * d:4 * d], mod_ref[:, 4 * d:5 * d])
        h_ref[...] = h.astype(BF16)
        acc_ref[...] = jnp.zeros(acc_ref.shape, F32)
        if routed:
            lane = lax.broadcasted_iota(I32, comb_ref.shape, 1).astype(F32)
            logits = jnp.where(lane < N_EXPERTS, _mm3(h, router_ref[...]), -jnp.inf)
            m1 = jnp.max(logits, axis=1, keepdims=True)
            i1 = jnp.min(jnp.where(logits == m1, lane, float(LANES)), axis=1, keepdims=True)
            rest = jnp.where(lane == i1, -jnp.inf, logits)
            m2 = jnp.max(rest, axis=1, keepdims=True)
            i2 = jnp.min(jnp.where(rest == m2, lane, float(LANES)), axis=1, keepdims=True)
            e2 = jnp.exp(m2 - m1)
            comb_ref[...] = (jnp.where(lane == i1, 1.0 / (1.0 + e2), 0.0)
                             + jnp.where(lane == i2, e2 / (1.0 + e2), 0.0))

    h = h_ref[...]
    t = _silu(jnp.dot(h, wg_ref[...], preferred_element_type=F32)) * jnp.dot(
        h, wu_ref[...], preferred_element_type=F32)
    y = jnp.dot(t.astype(BF16), wd_ref[...], preferred_element_type=F32)
    if routed:
        lane = lax.broadcasted_iota(I32, comb_ref.shape, 1)
        y = y * jnp.sum(jnp.where(lane == j, comb_ref[...], 0.0), axis=1, keepdims=True)
    acc_ref[...] += y

    @pl.when(j == pl.num_programs(2) - 1)
    def _():
        y_ref[...] = x_ref[...] + mod_ref[:, 5 * d:6 * d] * acc_ref[...]


def ffn(x, norm_w, mod, router_pad, wg, wu, wd, tm, routed):
    g, r, d = x.shape
    nj, _, f = wg.shape
    tok = pl.BlockSpec((None, tm, d), lambda g, i, j: (g, i, 0))
    if mod.shape[1] == 1:
        mod_spec = pl.BlockSpec((None, 1, mod.shape[2]), lambda g, i, j: (g, 0, 0))
    else:
        mod_spec = pl.BlockSpec((None, tm, mod.shape[2]), lambda g, i, j: (g, i, 0))
    return pl.pallas_call(
        functools.partial(_ffn_kernel, routed),
        grid=(g, r // tm, nj),
        in_specs=[tok, pl.BlockSpec((1, d), lambda g, i, j: (0, 0)), mod_spec,
                  pl.BlockSpec(router_pad.shape, lambda g, i, j: (0, 0)),
                  pl.BlockSpec((None, d, f), lambda g, i, j: (j, 0, 0)),
                  pl.BlockSpec((None, d, f), lambda g, i, j: (j, 0, 0)),
                  pl.BlockSpec((None, f, d), lambda g, i, j: (j, 0, 0))],
        out_specs=tok,
        out_shape=jax.ShapeDtypeStruct(x.shape, F32),
        scratch_shapes=[pltpu.VMEM((tm, d), BF16), pltpu.VMEM((tm, d), F32), pltpu.VMEM((tm, LANES), F32)],
        compiler_params=_cparams("parallel", "parallel", "arbitrary"),
        name="ffn_routed" if routed else "ffn_dense",
    )(x, norm_w.reshape(1, d), mod, router_pad, wg, wu, wd)


def _layer_params(l, p):
    z = jnp.zeros((LORA_W, RWKV_W), F32)
    w_in = jnp.pad(p["w_in"][l], ((0, 0), (0, SHIFT_W + ATTN_IN_W + IDX_PAD_W - IN_W))).astype(BF16)
    j = l // 2
    if l % 2 == 0:
        halves = lambda w: jnp.stack(jnp.split(w, w.shape[1] // D_FF_TILE, axis=1)).astype(BF16)
        wg, wu = halves(p["ffn_w_gate"][j]), halves(p["ffn_w_up"][j])
        wd = jnp.stack(jnp.split(p["ffn_w_down"][j], wg.shape[0], axis=0)).astype(BF16)
        router = jnp.zeros((8, LANES), F32)
    else:
        wg, wu, wd = (p[n][j].astype(BF16) for n in ("exp_w_gate", "exp_w_up", "exp_w_down"))
        router = jnp.pad(p["router_w"][j], ((0, 0), (0, LANES - N_EXPERTS)))
    return dict(
        norm1_w=p["norm1_w"][l], norm2_w=p["norm2_w"][l], w_in=w_in, mu=p["mu_shift"][l][None, :],
        vecs=jnp.stack([p["rwkv_w0"][l], p["rwkv_a0"][l], p["rwkv_k_k"][l], p["rwkv_k_a"][l],
                        p["rwkv_r_k"][l].reshape(RWKV_W)] + [jnp.zeros((RWKV_W,), F32)] * 3),
        w2=z.at[0:W_LORA].set(p["rwkv_w2"][l]),
        a2=z.at[W_LORA:W_LORA + A_LORA].set(p["rwkv_a2"][l]),
        g2=z.at[W_LORA + A_LORA:].set(p["rwkv_g2"][l]),
        lnx=jnp.stack([p["lnx_w"][l], p["lnx_b"][l]]),
        qkw=jnp.stack([jnp.tile(p["q_norm_w"][l], ATTN_H), jnp.tile(p["k_norm_w"][l], ATTN_H)]),
        w_out=p["w_out"][l].astype(BF16), router=router, wg=wg, wu=wu, wd=wd, routed=(l % 2 == 1))


def _trunk_layer(x, mod, lp, shift_prev, s0, tabs, seq_mode, tm, attend):
    g, r, _ = x.shape
    pr, pa, pi = proj_in(x, lp["norm1_w"], mod, lp["w_in"], tm)
    dr, dd, k2, vr, kk, bb_, gate, bonus, kr = rwkv_prep(pr, shift_prev, lp["mu"], lp["vecs"], lp["w2"],
                                                         lp["a2"], lp["g2"], tm, seq_mode)
    row_operands = (dr, dd, k2, kk, bb_, kr)
    if seq_mode:
        v_t = vr.reshape(g, r, RWKV_H, HEAD_DIM).transpose(0, 2, 3, 1)
        o_tiles, s_fin = rwkv_scan(row_operands, v_t, _pack_state(s0), g, LANES)
        o_scan = _readout_rows(o_tiles, r)
    else:
        o_tiles, s_fin = rwkv_scan([a.reshape(r, 1, RWKV_W) for a in row_operands],
                                   vr.reshape(r, RWKV_H, HEAD_DIM, 1), _pack_state(s0), 8, 1)
        o_scan = _readout_rows(o_tiles, 1).reshape(g, r, RWKV_W)
    q, k, qi, ki = attn_prep(pa, pi, lp["qkw"], tabs[0], tabs[1], tm)
    v = pa[..., 2 * ATTN_W:]
    kidx = ki[..., :IDX_D]
    widx = ki[..., IDX_D:IDX_D + IDX_H]
    o_attn = attend(q, k, v, qi, kidx, widx)
    x = proj_out(x, o_scan, bonus, gate, o_attn, lp["lnx"], mod, lp["w_out"], tm)
    x = ffn(x, lp["norm2_w"], mod, lp["router"], lp["wg"], lp["wu"], lp["wd"], min(512, r), lp["routed"])
    return x, (k, v, kidx, _unpack_state(s_fin), pr[:, -1] if seq_mode else pr[0])


def kernel(x_prompt, x_sample, c_prompt, c_sample, cache_k, cache_v, cache_kidx, state_wkv, state_shift, page_table, norm1_w, norm2_w, w_ada, b_ada, w_in, mu_shift, rwkv_w0, rwkv_w2, rwkv_a0, rwkv_a2, rwkv_g2, rwkv_k_k, rwkv_k_a, rwkv_r_k, lnx_w, lnx_b, q_norm_w, k_norm_w, w_out, ffn_w_gate, ffn_w_up, ffn_w_down, router_w, exp_w_gate, exp_w_up, exp_w_down):
    params = dict(norm1_w=norm1_w, norm2_w=norm2_w, w_in=w_in, mu_shift=mu_shift, rwkv_w0=rwkv_w0,
                  rwkv_w2=rwkv_w2, rwkv_a0=rwkv_a0, rwkv_a2=rwkv_a2, rwkv_g2=rwkv_g2, rwkv_k_k=rwkv_k_k,
                  rwkv_k_a=rwkv_k_a, rwkv_r_k=rwkv_r_k, lnx_w=lnx_w, lnx_b=lnx_b, q_norm_w=q_norm_w,
                  k_norm_w=k_norm_w, w_out=w_out, ffn_w_gate=ffn_w_gate, ffn_w_up=ffn_w_up,
                  ffn_w_down=ffn_w_down, router_w=router_w, exp_w_gate=exp_w_gate, exp_w_up=exp_w_up,
                  exp_w_down=exp_w_down)
    depth = w_in.shape[0]
    bp, t, d = x_prompt.shape
    nb = x_sample.shape[0]
    n_pages = page_table.shape[1]
    past = n_pages * PAGE_SIZE
    pt_flat = page_table.reshape(-1).astype(I32)
    cache_k_t = jnp.transpose(cache_k, (0, 1, 3, 4, 2))
    cache_v_t = jnp.transpose(cache_v, (0, 1, 3, 4, 2))
    cache_kidx_t = jnp.transpose(cache_kidx, (0, 1, 3, 2))

    n_mod = bp + nb
    c_all = jnp.pad(jnp.concatenate([c_prompt, c_sample]), ((0, -n_mod % 8), (0, 0)))
    mod = ada_modulation(c_all, w_ada, b_ada)
    mod_p = mod[:, :bp, None, :]
    mod_s = mod[:, None, bp:n_mod, :]

    def tables(pos):
        return (rope_tables(pos, ATTN_W, HEAD_DIM, ROT_DIM, ATTN_W),
                rope_tables(pos, IDX_PAD_W, IDX_D, IDX_ROT_DIM, IDX_QW + IDX_D))

    tabs_p = tables(jnp.arange(t, dtype=I32))
    tabs_s = tables(jnp.full((1,), past, I32))
    tm_p = min(256, t)
    tq, tk = min(128, t), min(512, t)

    xp, xs = x_prompt, x_sample.reshape(1, nb, d)
    st_p, st_s = [], []
    for l in range(depth):
        lp = _layer_params(l, params)
        attend_p = lambda q, k, v, qi, ki, wi: dsa_prompt(q, k, v, qi, ki, wi, tq, tk)

        def attend_s(q, k, v, qi, ki, wi, l=l):
            rows = lambda a: a.reshape(nb, 1, ATTN_W)
            qi_h = qi.reshape(nb, IDX_H, IDX_D)
            scores = sample_scores(l, pt_flat, cache_kidx_t, qi_h, wi.reshape(nb, IDX_H, 1), n_pages)
            w_pad = jnp.pad(wi.reshape(nb, IDX_H), ((0, 0), (0, LANES - IDX_H)))
            bias, bias_self = sample_select(scores.reshape(nb, past), qi.reshape(nb, IDX_QW),
                                            jnp.tile(ki.reshape(nb, IDX_D), (1, IDX_H)), w_pad)
            o = sample_attend(l, pt_flat, cache_k_t, cache_v_t, bias.reshape(nb, 1, past),
                              bias_self.reshape(nb, 1, LANES), rows(q), rows(k), rows(v), n_pages)
            return o.reshape(1, nb, ATTN_W)

        xp, sp = _trunk_layer(xp, mod_p[l], lp, jnp.zeros((bp, 1, SHIFT_W), F32),
                              jnp.zeros((bp, RWKV_H, HEAD_DIM, HEAD_DIM), F32), tabs_p, True, tm_p, attend_p)
        xs, ss = _trunk_layer(xs, mod_s[l], lp, state_shift[l][None], state_wkv[l], tabs_s, False, nb, attend_s)
        st_p.append(sp)
        st_s.append(ss)

    def stack(states, i, shape):
        return jnp.stack([s[i] for s in states]).reshape(shape)

    return (xp, xs.reshape(nb, 1, d),
            stack(st_p, 0, (depth, bp, t, ATTN_H, HEAD_DIM)), stack(st_p, 1, (depth, bp, t, ATTN_H, HEAD_DIM)),
            stack(st_p, 2, (depth, bp, t, IDX_D)), stack(st_p, 3, (depth, bp, RWKV_H, HEAD_DIM, HEAD_DIM)),
            stack(st_p, 4, (depth, bp, SHIFT_W)),
            stack(st_s, 0, (depth, nb, 1, ATTN_H, HEAD_DIM)), stack(st_s, 1, (depth, nb, 1, ATTN_H, HEAD_DIM)),
            stack(st_s, 2, (depth, nb, 1, IDX_D)), stack(st_s, 3, (depth, nb, RWKV_H, HEAD_DIM, HEAD_DIM)),
            stack(st_s, 4, (depth, nb, SHIFT_W)))
```
